```python
import math
import jax, jax.numpy as jnp
from jax import lax
import numpy as np

D_MODEL = 1024
BATCH = 32
SEQ = 2048
DEPTH = 2

CHUNK = 64
N_META = 16
SSD_HEADS = 16
SSD_HEAD_DIM = 64
SSD_INNER = SSD_HEADS * SSD_HEAD_DIM
SSD_GROUPS = 4
SSD_STATE = 128
SSD_CONV = 4
S5_WIDTH = D_MODEL // 2
S5_GROUP = 16
S5_GROUPS = S5_WIDTH // S5_GROUP
S5_STATE = 64
D_FF = 4 * D_MODEL
EPS = 1e-6

XBC_WIDTH = SSD_INNER + 2 * SSD_GROUPS * SSD_STATE
IN_SPLITS = (SSD_INNER, SSD_INNER + XBC_WIDTH, SSD_INNER + XBC_WIDTH + SSD_HEADS,
             SSD_INNER + XBC_WIDTH + SSD_HEADS + S5_WIDTH)
IN_WIDTH = IN_SPLITS[-1] + 2 * D_MODEL

kernel_name = "hybrid_ssd_s5_gated_encoder"


def rmsnorm(x, w):
    xf = x.astype(jnp.float32)
    xf = xf * lax.rsqrt(jnp.mean(xf * xf, axis=-1, keepdims=True) + EPS)
    return (xf * w.astype(jnp.float32)).astype(x.dtype)


def causal_dwconv(u, w, b):
    k, c = w.shape
    y = lax.conv_general_dilated(u, w[:, None, :].astype(u.dtype), window_strides=(1,),
                                 padding=[(k - 1, 0)], dimension_numbers=("NWC", "WIO", "NWC"),
                                 feature_group_count=c)
    return y + b.astype(u.dtype)


def ssd_chunked(xs, dt, a, bm, cm):
    b, seq_len, n_heads, p = xs.shape
    g, n = bm.shape[-2:]
    r = n_heads // g
    nc = seq_len // CHUNK
    dt_c = dt.reshape(b, nc, CHUNK, g, r)
    xdt = xs.reshape(b, nc, CHUNK, g, r, p) * dt_c[..., None]
    bc = bm.reshape(b, nc, CHUNK, g, n)
    cc = cm.reshape(b, nc, CHUNK, g, n)
    da_cs = jnp.cumsum(dt_c * a.reshape(g, r), axis=2)
    causal = jnp.tril(jnp.ones((CHUNK, CHUNK), dtype=bool))[:, :, None, None]
    seg = da_cs[:, :, :, None] - da_cs[:, :, None, :]
    decay = jnp.exp(jnp.where(causal, seg, -jnp.inf))
    scores = jnp.einsum("bclgn,bcsgn->bclsg", cc, bc)
    y_diag = jnp.einsum("bclsg,bclsgr,bcsgrp->bclgrp", scores, decay, xdt)
    decay_states = jnp.exp(da_cs[:, :, -1:] - da_cs)
    states = jnp.einsum("bclgn,bclgr,bclgrp->bcgrpn", bc, decay_states, xdt)
    chunk_decay = jnp.exp(da_cs[:, :, -1])

    def step(h, inp):
        s, d = inp
        return h * d[..., None, None] + s, h

    h0 = jnp.zeros((b, g, r, p, n), xs.dtype)
    _, h_prev = lax.scan(step, h0, (jnp.moveaxis(states, 1, 0), jnp.moveaxis(chunk_decay, 1, 0)))
    h_prev = jnp.moveaxis(h_prev, 0, 1)
    y_off = jnp.einsum("bclgn,bcgrpn,bclgr->bclgrp", cc, h_prev, jnp.exp(da_cs))
    return (y_diag + y_off).reshape(b, seq_len, n_heads, p)


def ssd_branch(z, xbc_raw, dt_raw, conv_w, conv_b, dt_bias, a_log, d_skip, norm_w):
    b, seq_len, _ = z.shape
    xbc = jax.nn.silu(causal_dwconv(xbc_raw, conv_w, conv_b)).astype(jnp.float32)
    xs = xbc[..., :SSD_INNER].reshape(b, seq_len, SSD_HEADS, SSD_HEAD_DIM)
    bm = xbc[..., SSD_INNER:SSD_INNER + SSD_GROUPS * SSD_STATE].reshape(b, seq_len, SSD_GROUPS, SSD_STATE)
    cm = xbc[..., SSD_INNER + SSD_GROUPS * SSD_STATE:].reshape(b, seq_len, SSD_GROUPS, SSD_STATE)
    dt = jax.nn.softplus(dt_raw.astype(jnp.float32) + dt_bias.astype(jnp.float32))
    a = -jnp.exp(a_log.astype(jnp.float32))
    pad = (-seq_len) % CHUNK
    padf = lambda t: jnp.pad(t, [(0, 0), (pad, 0)] + [(0, 0)] * (t.ndim - 2))
    y = ssd_chunked(padf(xs), padf(dt), a, padf(bm), padf(cm))[:, pad:]
    y = y + d_skip.astype(jnp.float32)[:, None] * xs
    y = y.reshape(b, seq_len, SSD_INNER) * jax.nn.silu(z.astype(jnp.float32))
    yg = y.reshape(b, seq_len, SSD_GROUPS, SSD_INNER // SSD_GROUPS)
    yg = yg * lax.rsqrt(jnp.mean(yg * yg, axis=-1, keepdims=True) + EPS)
    return (yg.reshape(b, seq_len, SSD_INNER) * norm_w.astype(jnp.float32)).astype(z.dtype)


def s5_branch(u, a_re, a_im, log_step, b_re, b_im, c_re, c_im, d_skip, w_glu):
    bsz, seq_len, _ = u.shape
    uf = u.astype(jnp.float32).reshape(bsz, seq_len, S5_GROUPS, S5_GROUP)
    step = jnp.exp(log_step.astype(jnp.float32))[:, None]
    lam_re = a_re.astype(jnp.float32)
    lam_im = a_im.astype(jnp.float32)
    mag = jnp.exp(lam_re * step)
    ab_re = mag * jnp.cos(lam_im * step)
    ab_im = mag * jnp.sin(lam_im * step)
    den = lam_re * lam_re + lam_im * lam_im
    nr = ab_re - 1.0
    f_re = (nr * lam_re + ab_im * lam_im) / den
    f_im = (ab_im * lam_re - nr * lam_im) / den
    br = b_re.astype(jnp.float32)
    bi = b_im.astype(jnp.float32)
    bb_re = f_re[..., None] * br - f_im[..., None] * bi
    bb_im = f_re[..., None] * bi + f_im[..., None] * br
    bu_re = jnp.einsum("bltc,tnc->bltn", uf, bb_re)
    bu_im = jnp.einsum("bltc,tnc->bltn", uf, bb_im)
    a_seq_re = jnp.broadcast_to(ab_re[None, None], (1, seq_len, S5_GROUPS, S5_STATE))
    a_seq_im = jnp.broadcast_to(ab_im[None, None], (1, seq_len, S5_GROUPS, S5_STATE))

    def combine(left, right):
        ar1, ai1, br1, bi1 = left
        ar2, ai2, br2, bi2 = right
        return (ar2 * ar1 - ai2 * ai1, ar2 * ai1 + ai2 * ar1,
                ar2 * br1 - ai2 * bi1 + br2, ar2 * bi1 + ai2 * br1 + bi2)

    _, _, h_re, h_im = lax.associative_scan(combine, (a_seq_re, a_seq_im, bu_re, bu_im), axis=1)
    y = (jnp.einsum("bltn,tcn->bltc", h_re, c_re.astype(jnp.float32))
         - jnp.einsum("bltn,tcn->bltc", h_im, c_im.astype(jnp.float32)))
    y = y + d_skip.astype(jnp.float32).reshape(S5_GROUPS, S5_GROUP) * uf
    y = jax.nn.gelu(y.reshape(bsz, seq_len, S5_WIDTH)).astype(u.dtype)
    val, gate = jnp.split(y @ w_glu, 2, axis=-1)
    return val * jax.nn.sigmoid(gate)


def setup_inputs(seed: int = 0) -> dict:
    key = jax.random.key(seed)
    ks = jax.random.split(key, 26)
    f32 = jnp.float32
    nrm = lambda k, shape, s: jax.random.normal(k, shape, f32) * s
    dt0 = jnp.exp(jax.random.uniform(ks[5], (DEPTH, SSD_HEADS), f32, math.log(1e-3), math.log(1e-1)))
    return {
        "x": nrm(ks[0], (BATCH, SEQ, D_MODEL), 1.0),
        "meta_tokens": nrm(ks[1], (N_META, D_MODEL), 1.0),
        "w_in": nrm(ks[2], (DEPTH, D_MODEL, IN_WIDTH), D_MODEL ** -0.5),
        "conv_w": nrm(ks[3], (DEPTH, SSD_CONV, XBC_WIDTH), SSD_CONV ** -0.5),
        "conv_b": nrm(ks[4], (DEPTH, XBC_WIDTH), 0.01),
        "dt_bias": dt0 + jnp.log(-jnp.expm1(-dt0)),
        "ssd_a_log": jnp.log(jax.random.uniform(ks[6], (DEPTH, SSD_HEADS), f32, 1.0, 16.0)),
        "ssd_d": 1.0 + nrm(ks[7], (DEPTH, SSD_HEADS), 0.1),
        "ssd_norm_w": 1.0 + nrm(ks[8], (DEPTH, SSD_INNER), 0.02),
        "s5_a_re": -0.5 + nrm(ks[9], (DEPTH, S5_GROUPS, S5_STATE), 0.01),
        "s5_a_im": jnp.pi * jnp.arange(S5_STATE, dtype=f32) + nrm(ks[10], (DEPTH, S5_GROUPS, S5_STATE), 0.01),
        "s5_log_step": jax.random.uniform(ks[11], (DEPTH, S5_GROUPS), f32, math.log(1e-3), math.log(1e-1)),
        "s5_b_re": nrm(ks[12], (DEPTH, S5_GROUPS, S5_STATE, S5_GROUP), (2 * S5_GROUP) ** -0.5),
        "s5_b_im": nrm(ks[13], (DEPTH, S5_GROUPS, S5_STATE, S5_GROUP), (2 * S5_GROUP) ** -0.5),
        "s5_c_re": nrm(ks[14], (DEPTH, S5_GROUPS, S5_GROUP, S5_STATE), S5_STATE ** -0.5),
        "s5_c_im": nrm(ks[15], (DEPTH, S5_GROUPS, S5_GROUP, S5_STATE), S5_STATE ** -0.5),
        "s5_d": nrm(ks[16], (DEPTH, S5_WIDTH), 1.0),
        "w_glu": nrm(ks[17], (DEPTH, S5_WIDTH, 2 * D_MODEL), S5_WIDTH ** -0.5),
        "w_out": nrm(ks[18], (DEPTH, D_MODEL, D_MODEL), D_MODEL ** -0.5),
        "norm_mix_w": 1.0 + nrm(ks[19], (DEPTH, D_MODEL), 0.02),
        "norm_mlp_w": 1.0 + nrm(ks[20], (DEPTH, D_MODEL), 0.02),
        "w_ff_in": nrm(ks[21], (DEPTH, D_MODEL, D_FF), D_MODEL ** -0.5),
        "w_ff_out": nrm(ks[22], (DEPTH, D_FF, D_MODEL), D_FF ** -0.5),
        "final_norm_w": 1.0 + nrm(ks[23], (D_MODEL,), 0.02),
    }


def reference(x, meta_tokens, w_in, conv_w, conv_b, dt_bias, ssd_a_log, ssd_d, ssd_norm_w,
              s5_a_re, s5_a_im, s5_log_step, s5_b_re, s5_b_im, s5_c_re, s5_c_im, s5_d, w_glu,
              w_out, norm_mix_w, norm_mlp_w, w_ff_in, w_ff_out, final_norm_w):
    bsz = x.shape[0]
    meta = jnp.broadcast_to(meta_tokens[None].astype(x.dtype), (bsz, N_META, D_MODEL))
    h = jnp.concatenate([meta, x], axis=1)
    for i in range(DEPTH):
        xn = rmsnorm(h, norm_mix_w[i])
        z, xbc, dt_raw, u_s5, gates = jnp.split(xn @ w_in[i], IN_SPLITS, axis=-1)
        y_a = ssd_branch(z, xbc, dt_raw, conv_w[i], conv_b[i], dt_bias[i], ssd_a_log[i], ssd_d[i], ssd_norm_w[i])
        y_b = s5_branch(u_s5, s5_a_re[i], s5_a_im[i], s5_log_step[i], s5_b_re[i], s5_b_im[i],
                        s5_c_re[i], s5_c_im[i], s5_d[i], w_glu[i])
        g_a, g_b = jnp.split(jax.nn.sigmoid(gates), 2, axis=-1)
        h = h + ((g_a * y_a + g_b * y_b) @ w_out[i]).astype(h.dtype)
        hn = rmsnorm(h, norm_mlp_w[i])
        h = h + jnp.square(jax.nn.relu(hn @ w_ff_in[i])) @ w_ff_out[i]
    h = rmsnorm(h, final_norm_w)
    return h[:, N_META:]
```

```python
import functools
import math

import jax
import jax.numpy as jnp
from jax import lax
from jax.experimental import pallas as pl
from jax.experimental.pallas import tpu as pltpu

F32 = jnp.float32
BF16 = jnp.bfloat16

D_MODEL = 1024
BATCH = 32
SEQ = 2048
DEPTH = 2
N_META = 16
SSD_HEADS = 16
SSD_HEAD_DIM = 64
SSD_INNER = SSD_HEADS * SSD_HEAD_DIM
SSD_GROUPS = 4
SSD_HEADS_PER_GROUP = SSD_HEADS // SSD_GROUPS
SSD_STATE = 128
SSD_CONV = 4
S5_WIDTH = D_MODEL // 2
S5_GROUP = 16
S5_GROUPS = S5_WIDTH // S5_GROUP
S5_STATE = 64
S5_LANES = S5_GROUPS * S5_STATE
D_FF = 4 * D_MODEL
EPS = 1e-6
XBC_WIDTH = SSD_INNER + 2 * SSD_GROUPS * SSD_STATE

LANE = 128
SUBLANE = 8

SSD_T = 128
PAD_ROWS = SSD_T - N_META
LP = PAD_ROWS + N_META + SEQ
M_TOT = BATCH * LP
S5_TC = SUBLANE
S5_ROWS = BATCH * S5_TC
S5_SLABS = 2 * S5_LANES // LANE

PROJ_W = XBC_WIDTH + 2 * D_MODEL + SSD_INNER + S5_WIDTH
TN_PROJ = 512
TM_TOK = 1024
TM_MERGE = 512
TF_MLP = 512
VMEM_LIMIT = 48 * 1024 * 1024


def _sigmoid(x):
    return 1.0 / (1.0 + jnp.exp(-x))


def _rms_scale(x):
    return x * lax.rsqrt(jnp.mean(x * x, axis=-1, keepdims=True) + EPS)


def _bdot(a, b):
    return jnp.dot(a, b, preferred_element_type=F32)


def _inproj_kernel(h_ref, nw_ref, w_ref, wdt_ref, proj_ref, dt_ref, xn_ref):
    @pl.when(pl.program_id(1) == 0)
    def _():
        xn = (_rms_scale(h_ref[...]) * nw_ref[...]).astype(BF16)
        xn_ref[...] = xn
        dt_ref[...] = _bdot(xn, wdt_ref[...])

    proj_ref[...] = _bdot(xn_ref[...], w_ref[...])


def _inproj(h, norm_w, w_main, w_dt):
    return pl.pallas_call(
        _inproj_kernel,
        grid=(M_TOT // TM_TOK, PROJ_W // TN_PROJ),
        in_specs=[
            pl.BlockSpec((TM_TOK, D_MODEL), lambda i, j: (i, 0)),
            pl.BlockSpec((1, D_MODEL), lambda i, j: (0, 0)),
            pl.BlockSpec((D_MODEL, TN_PROJ), lambda i, j: (0, j)),
            pl.BlockSpec((D_MODEL, LANE), lambda i, j: (0, 0)),
        ],
        out_specs=[
            pl.BlockSpec((TM_TOK, TN_PROJ), lambda i, j: (i, j)),
            pl.BlockSpec((TM_TOK, LANE), lambda i, j: (i, 0)),
        ],
        out_shape=[
            jax.ShapeDtypeStruct((M_TOT, PROJ_W), F32),
            jax.ShapeDtypeStruct((M_TOT, LANE), F32),
        ],
        scratch_shapes=[pltpu.VMEM((TM_TOK, D_MODEL), BF16)],
        compiler_params=pltpu.CompilerParams(
            dimension_semantics=("arbitrary", "arbitrary"), vmem_limit_bytes=VMEM_LIMIT),
        name="inproj",
    )(h, norm_w, w_main, w_dt)


def _ssd_kernel(xbc_ref, z_ref, dt_ref, cw_ref, cb_ref, dtb_ref, alog_ref, dsk_ref, nw_ref,
                y_ref, ext_ref, act_ref, yacc_ref, st_ref):
    c = pl.program_id(1)
    t = SSD_T

    @pl.when(c == 0)
    def _():
        ext_ref[0:SUBLANE, :] = jnp.zeros((SUBLANE, XBC_WIDTH), F32)
        st_ref[...] = jnp.zeros_like(st_ref)

    ext_ref[SUBLANE:SUBLANE + t, :] = xbc_ref[...]
    for j in range(XBC_WIDTH // LANE):
        ls = slice(j * LANE, (j + 1) * LANE)
        acc = jnp.broadcast_to(cb_ref[:, ls], (t, LANE))
        for k in range(SSD_CONV):
            off = SUBLANE - (SSD_CONV - 1) + k
            acc = acc + cw_ref[k:k + 1, ls] * ext_ref[off:off + t, ls]
        act_ref[:, ls] = acc * _sigmoid(acc)
    ext_ref[0:SUBLANE, :] = ext_ref[t:t + SUBLANE, :]

    xdt = dt_ref[...] + dtb_ref[...]
    dt = jnp.maximum(xdt, 0.0) + jnp.log1p(jnp.exp(-jnp.abs(xdt)))
    row = c * t + lax.broadcasted_iota(jnp.int32, (t, LANE), 0)
    dt = jnp.where(row >= PAD_ROWS, dt, 0.0)
    a = -jnp.exp(alog_ref[...])
    r_i = lax.broadcasted_iota(jnp.int32, (t, t), 0)
    c_i = lax.broadcasted_iota(jnp.int32, (t, t), 1)
    causal = r_i >= c_i
    cs = jnp.dot(causal.astype(F32), dt * a, preferred_element_type=F32,
                 precision=lax.Precision.HIGHEST)
    cs_t = cs.T
    exp_cs = jnp.exp(cs)
    dec_end_t = jnp.exp(cs_t[:, t - 1:t] - cs_t)
    chunk_dec = jnp.exp(cs_t[:, t - 1:t])

    for g in range(SSD_GROUPS):
        b_g = act_ref[:, SSD_INNER + g * SSD_STATE:SSD_INNER + (g + 1) * SSD_STATE]
        c_lo = SSD_INNER + SSD_GROUPS * SSD_STATE + g * SSD_STATE
        c_g = act_ref[:, c_lo:c_lo + SSD_STATE]
        b_gt = b_g.T
        scores = _bdot(c_g.astype(BF16), b_gt.astype(BF16))
        for r in range(SSD_HEADS_PER_GROUP):
            h = g * SSD_HEADS_PER_GROUP + r
            hs = slice(h * SSD_HEAD_DIM, (h + 1) * SSD_HEAD_DIM)
            seg = cs[:, h:h + 1] - cs_t[h:h + 1, :]
            decay = jnp.exp(jnp.where(causal, seg, -jnp.inf))
            xs_h = act_ref[:, hs]
            xdt_h = (xs_h * dt[:, h:h + 1]).astype(BF16)
            y_diag = _bdot((scores * decay).astype(BF16), xdt_h)
            st_h = st_ref[h]
            y_off = _bdot((c_g * exp_cs[:, h:h + 1]).astype(BF16), st_h.astype(BF16))
            yacc_ref[:, hs] = y_diag + y_off
            b_dec_t = (b_gt * dec_end_t[h:h + 1, :]).astype(BF16)
            st_ref[h] = st_h * chunk_dec[h:h + 1, :] + _bdot(b_dec_t, xdt_h)

    gw = SSD_INNER // SSD_GROUPS
    for g in range(SSD_GROUPS):
        gs = slice(g * gw, (g + 1) * gw)
        z = z_ref[:, gs]
        y = (yacc_ref[:, gs] + dsk_ref[:, gs] * act_ref[:, gs]) * (z * _sigmoid(z))
        y_ref[:, gs] = _rms_scale(y) * nw_ref[:, gs]


def _ssd(proj3, dt3, conv_w, conv_b, dt_bias, a_log, d_full, norm_w):
    nc = LP // SSD_T
    z_blk = (XBC_WIDTH + 2 * D_MODEL) // SSD_INNER
    const = lambda b, c: (0, 0)
    return pl.pallas_call(
        _ssd_kernel,
        grid=(BATCH, nc),
        in_specs=[
            pl.BlockSpec((None, SSD_T, XBC_WIDTH), lambda b, c: (b, c, 0)),
            pl.BlockSpec((None, SSD_T, SSD_INNER), lambda b, c: (b, c, z_blk)),
            pl.BlockSpec((None, SSD_T, LANE), lambda b, c: (b, c, 0)),
            pl.BlockSpec((SSD_CONV, XBC_WIDTH), const),
            pl.BlockSpec((1, XBC_WIDTH), const),
            pl.BlockSpec((1, LANE), const),
            pl.BlockSpec((1, LANE), const),
            pl.BlockSpec((1, SSD_INNER), const),
            pl.BlockSpec((1, SSD_INNER), const),
        ],
        out_specs=pl.BlockSpec((None, SSD_T, SSD_INNER), lambda b, c: (b, c, 0)),
        out_shape=jax.ShapeDtypeStruct((BATCH, LP, SSD_INNER), F32),
        scratch_shapes=[
            pltpu.VMEM((SSD_T + SUBLANE, XBC_WIDTH), F32),
            pltpu.VMEM((SSD_T, XBC_WIDTH), F32),
            pltpu.VMEM((SSD_T, SSD_INNER), F32),
            pltpu.VMEM((SSD_HEADS, SSD_STATE, SSD_HEAD_DIM), F32),
        ],
        compiler_params=pltpu.CompilerParams(
            dimension_semantics=("arbitrary", "arbitrary"), vmem_limit_bytes=VMEM_LIMIT),
        name="ssd",
    )(proj3, proj3, dt3, conv_w, conv_b, dt_bias, a_log, d_full, norm_w)


def _s5_disc_kernel(are_ref, aim_ref, ls_ref, bre_ref, bim_ref, abre_ref, abim_ref, bbre_ref, bbim_ref):
    step = jnp.exp(ls_ref[...])
    lam_re = are_ref[...]
    lam_im = aim_ref[...]
    mag = jnp.exp(lam_re * step)
    ab_re = mag * jnp.cos(lam_im * step)
    ab_im = mag * jnp.sin(lam_im * step)
    den = lam_re * lam_re + lam_im * lam_im
    nr = ab_re - 1.0
    f_re = (nr * lam_re + ab_im * lam_im) / den
    f_im = (ab_im * lam_re - nr * lam_im) / den
    br = bre_ref[...]
    bi = bim_ref[...]
    abre_ref[...] = ab_re
    abim_ref[...] = ab_im
    bbre_ref[...] = f_re * br - f_im * bi
    bbim_ref[...] = f_re * bi + f_im * br


def _s5_disc(a_re, a_im, log_step, b_re_t, b_im_t):
    sds = jax.ShapeDtypeStruct
    return pl.pallas_call(
        _s5_disc_kernel,
        out_shape=[sds(a_re.shape, F32), sds(a_re.shape, F32), sds(b_re_t.shape, F32), sds(b_re_t.shape, F32)],
        name="s5_disc",
    )(a_re, a_im, log_step, b_re_t, b_im_t)


def _s5_kernel(u_ref, bre_ref, bim_ref, are_ref, aim_ref, cre_ref, cim_ref, y_ref, slab_ref, h_ref):
    @pl.when(pl.program_id(0) == 0)
    def _():
        h_ref[...] = jnp.zeros_like(h_ref)

    u = u_ref[...].reshape(S5_ROWS, S5_WIDTH).astype(BF16)
    n_half = bre_ref.shape[0]
    kw = S5_WIDTH // n_half
    nw = S5_LANES // n_half
    for k in range(n_half):
        uk = u[:, k * kw:(k + 1) * kw]
        bu_re = _bdot(uk, bre_ref[k])
        bu_im = _bdot(uk, bim_ref[k])
        for j in range(nw // LANE):
            s = k * (nw // LANE) + j
            slab_ref[s] = bu_re[:, j * LANE:(j + 1) * LANE]
            slab_ref[S5_SLABS // 2 + s] = bu_im[:, j * LANE:(j + 1) * LANE]

    for s in range(S5_SLABS // 2):
        ls = slice(s * LANE, (s + 1) * LANE)
        li = slice(S5_LANES + s * LANE, S5_LANES + (s + 1) * LANE)
        a_r = are_ref[:, ls]
        a_i = aim_ref[:, ls]
        h_r = h_ref[:, ls]
        h_i = h_ref[:, li]
        for t in range(S5_TC):
            rows = pl.ds(t, BATCH, stride=S5_TC)
            b_r = slab_ref[s, rows, :]
            b_i = slab_ref[S5_SLABS // 2 + s, rows, :]
            n_r = a_r * h_r - a_i * h_i + b_r
            n_i = a_r * h_i + a_i * h_r + b_i
            h_r, h_i = n_r, n_i
            slab_ref[s, rows, :] = h_r
            slab_ref[S5_SLABS // 2 + s, rows, :] = h_i
        h_ref[:, ls] = h_r
        h_ref[:, li] = h_i

    half = S5_SLABS // 2
    h_re = jnp.concatenate([slab_ref[s] for s in range(half)], axis=1).astype(BF16)
    h_im = jnp.concatenate([slab_ref[half + s] for s in range(half)], axis=1).astype(BF16)
    y = _bdot(h_re, cre_ref[...]) - _bdot(h_im, cim_ref[...])
    y_ref[...] = y.reshape(BATCH, S5_TC, S5_WIDTH)


def _s5(proj3, bre_bd, bim_bd, ab_re, ab_im, cre_bd, cim_bd):
    u_blk = (PROJ_W - S5_WIDTH) // S5_WIDTH
    c2 = lambda i: (0, 0)
    c3 = lambda i: (0, 0, 0)
    return pl.pallas_call(
        _s5_kernel,
        grid=(LP // S5_TC,),
        in_specs=[
            pl.BlockSpec((BATCH, S5_TC, S5_WIDTH), lambda i: (0, i, u_blk)),
            pl.BlockSpec(bre_bd.shape, c3),
            pl.BlockSpec(bim_bd.shape, c3),
            pl.BlockSpec((1, S5_LANES), c2),
            pl.BlockSpec((1, S5_LANES), c2),
            pl.BlockSpec((S5_LANES, S5_WIDTH), c2),
            pl.BlockSpec((S5_LANES, S5_WIDTH), c2),
        ],
        out_specs=pl.BlockSpec((BATCH, S5_TC, S5_WIDTH), lambda i: (0, i, 0)),
        out_shape=jax.ShapeDtypeStruct((BATCH, LP, S5_WIDTH), F32),
        scratch_shapes=[
            pltpu.VMEM((S5_SLABS, S5_ROWS, LANE), F32),
            pltpu.VMEM((BATCH, 2 * S5_LANES), F32),
        ],
        compiler_params=pltpu.CompilerParams(
            dimension_semantics=("arbitrary",), vmem_limit_bytes=VMEM_LIMIT),
        name="s5",
    )(proj3, bre_bd, bim_bd, ab_re, ab_im, cre_bd, cim_bd)


def _merge_kernel(g_ref, ya_ref, ys_ref, u_ref, h_ref, sd_ref, wglu_ref, wout_ref, o_ref):
    ys = ys_ref[...] + sd_ref[...] * u_ref[...]
    cdf = 0.5 * (1.0 + jnp.tanh(math.sqrt(2.0 / math.pi) * (ys + 0.044715 * (ys * ys * ys))))
    vg = _bdot((ys * cdf).astype(BF16), wglu_ref[...])
    yb = vg[:, :D_MODEL] * _sigmoid(vg[:, D_MODEL:])
    m = _sigmoid(g_ref[:, :D_MODEL]) * ya_ref[...] + _sigmoid(g_ref[:, D_MODEL:]) * yb
    o_ref[...] = h_ref[...] + _bdot(m.astype(BF16), wout_ref[...])


def _merge(proj, y_a, y_s5, h, s5_d, w_glu, w_out):
    tm = TM_MERGE
    g_blk = XBC_WIDTH // (2 * D_MODEL)
    u_blk = (PROJ_W - S5_WIDTH) // S5_WIDTH
    const = lambda i: (0, 0)
    return pl.pallas_call(
        _merge_kernel,
        grid=(M_TOT // tm,),
        in_specs=[
            pl.BlockSpec((tm, 2 * D_MODEL), lambda i: (i, g_blk)),
            pl.BlockSpec((tm, SSD_INNER), lambda i: (i, 0)),
            pl.BlockSpec((tm, S5_WIDTH), lambda i: (i, 0)),
            pl.BlockSpec((tm, S5_WIDTH), lambda i: (i, u_blk)),
            pl.BlockSpec((tm, D_MODEL), lambda i: (i, 0)),
            pl.BlockSpec((1, S5_WIDTH), const),
            pl.BlockSpec((S5_WIDTH, 2 * D_MODEL), const),
            pl.BlockSpec((D_MODEL, D_MODEL), const),
        ],
        out_specs=pl.BlockSpec((tm, D_MODEL), lambda i: (i, 0)),
        out_shape=jax.ShapeDtypeStruct((M_TOT, D_MODEL), F32),
        compiler_params=pltpu.CompilerParams(
            dimension_semantics=("arbitrary",), vmem_limit_bytes=VMEM_LIMIT),
        name="merge",
    )(proj, y_a, y_s5, proj, h, s5_d, w_glu, w_out)


def _mlp_kernel(h_ref, nw_ref, w1_ref, w2_ref, o_ref, xn_ref):
    @pl.when(pl.program_id(1) == 0)
    def _():
        x = h_ref[...]
        xn_ref[...] = (_rms_scale(x) * nw_ref[...]).astype(BF16)
        o_ref[...] = x

    hid = jnp.square(jnp.maximum(_bdot(xn_ref[...], w1_ref[...]), 0.0))
    o_ref[...] += _bdot(hid.astype(BF16), w2_ref[...])


def _mlp(h, norm_w, w1, w2):
    return pl.pallas_call(
        _mlp_kernel,
        grid=(M_TOT // TM_TOK, D_FF // TF_MLP),
        in_specs=[
            pl.BlockSpec((TM_TOK, D_MODEL), lambda i, f: (i, 0)),
            pl.BlockSpec((1, D_MODEL), lambda i, f: (0, 0)),
            pl.BlockSpec((D_MODEL, TF_MLP), lambda i, f: (0, f)),
            pl.BlockSpec((TF_MLP, D_MODEL), lambda i, f: (f, 0)),
        ],
        out_specs=pl.BlockSpec((TM_TOK, D_MODEL), lambda i, f: (i, 0)),
        out_shape=jax.ShapeDtypeStruct((M_TOT, D_MODEL), F32),
        scratch_shapes=[pltpu.VMEM((TM_TOK, D_MODEL), BF16)],
        compiler_params=pltpu.CompilerParams(
            dimension_semantics=("arbitrary", "arbitrary"), vmem_limit_bytes=VMEM_LIMIT),
        name="mlp",
    )(h, norm_w, w1, w2)


def _final_kernel(h_ref, nw_ref, o_ref):
    o_ref[...] = _rms_scale(h_ref[LP - SEQ:, :]) * nw_ref[...]


def _final(h3, norm_w):
    return pl.pallas_call(
        _final_kernel,
        grid=(BATCH,),
        in_specs=[
            pl.BlockSpec((None, LP, D_MODEL), lambda b: (b, 0, 0)),
            pl.BlockSpec((1, D_MODEL), lambda b: (0, 0)),
        ],
        out_specs=pl.BlockSpec((None, SEQ, D_MODEL), lambda b: (b, 0, 0)),
        out_shape=jax.ShapeDtypeStruct((BATCH, SEQ, D_MODEL), F32),
        compiler_params=pltpu.CompilerParams(
            dimension_semantics=("arbitrary",), vmem_limit_bytes=VMEM_LIMIT),
        name="final_norm",
    )(h3, norm_w)


def _pad_lanes(v):
    return jnp.pad(v, (0, LANE - v.shape[0])).reshape(1, LANE)


def _block_diag(blocks):
    n, r, c = blocks.shape
    eye = jnp.eye(n, dtype=blocks.dtype)
    return (blocks[:, :, None, :] * eye[:, None, :, None]).reshape(n * r, n * c)


def kernel(x, meta_tokens, w_in, conv_w, conv_b, dt_bias, ssd_a_log, ssd_d, ssd_norm_w, s5_a_re, s5_a_im, s5_log_step, s5_b_re, s5_b_im, s5_c_re, s5_c_im, s5_d, w_glu, w_out, norm_mix_w, norm_mlp_w, w_ff_in, w_ff_out, final_norm_w):
    bsz = x.shape[0]
    meta = jnp.broadcast_to(meta_tokens[None].astype(x.dtype), (bsz, N_META, D_MODEL))
    pad = jnp.zeros((bsz, PAD_ROWS, D_MODEL), x.dtype)
    h = jnp.concatenate([pad, meta, x], axis=1).reshape(M_TOT, D_MODEL)

    o_xbc = SSD_INNER
    o_dt = o_xbc + XBC_WIDTH
    o_u = o_dt + SSD_HEADS
    o_g = o_u + S5_WIDTH
    n_half = 2
    gph = S5_GROUPS // n_half

    for i in range(DEPTH):
        wi = w_in[i]
        w_main = jnp.concatenate(
            [wi[:, o_xbc:o_dt], wi[:, o_g:], wi[:, :o_xbc], wi[:, o_u:o_g]], axis=1).astype(BF16)
        w_dt = jnp.pad(wi[:, o_dt:o_u], ((0, 0), (0, LANE - SSD_HEADS))).astype(BF16)
        proj, dt_raw = _inproj(h, norm_mix_w[i].reshape(1, D_MODEL), w_main, w_dt)
        proj3 = proj.reshape(BATCH, LP, PROJ_W)

        y_a = _ssd(
            proj3, dt_raw.reshape(BATCH, LP, LANE), conv_w[i], conv_b[i].reshape(1, XBC_WIDTH),
            _pad_lanes(dt_bias[i]), _pad_lanes(ssd_a_log[i]),
            jnp.repeat(ssd_d[i], SSD_HEAD_DIM).reshape(1, SSD_INNER),
            ssd_norm_w[i].reshape(1, SSD_INNER))

        ab_re, ab_im, bb_re_t, bb_im_t = _s5_disc(
            s5_a_re[i].reshape(S5_GROUPS, 1, S5_STATE), s5_a_im[i].reshape(S5_GROUPS, 1, S5_STATE),
            s5_log_step[i].reshape(S5_GROUPS, 1, 1),
            jnp.swapaxes(s5_b_re[i], 1, 2), jnp.swapaxes(s5_b_im[i], 1, 2))
        bd_b = lambda bt: jnp.stack(
            [_block_diag(bt[k * gph:(k + 1) * gph]) for k in range(n_half)]).astype(BF16)
        bd_c = lambda cm: _block_diag(jnp.swapaxes(cm, 1, 2)).astype(BF16)
        y_s5 = _s5(
            proj3, bd_b(bb_re_t), bd_b(bb_im_t),
            ab_re.reshape(1, S5_LANES), ab_im.reshape(1, S5_LANES),
            bd_c(s5_c_re[i]), bd_c(s5_c_im[i]))

        h = _merge(proj, y_a.reshape(M_TOT, SSD_INNER), y_s5.reshape(M_TOT, S5_WIDTH), h,
                   s5_d[i].reshape(1, S5_WIDTH), w_glu[i].astype(BF16), w_out[i].astype(BF16))
        h = _mlp(h, norm_mlp_w[i].reshape(1, D_MODEL), w_ff_in[i].astype(BF16), w_ff_out[i].astype(BF16))

    return _final(h.reshape(BATCH, LP, D_MODEL), final_norm_w.reshape(1, D_MODEL))
```

```python
import functools
import math

import jax
import jax.numpy as jnp
from jax import lax
from jax.experimental import pallas as pl
from jax.experimental.pallas import tpu as pltpu

F32 = jnp.float32
BF16 = jnp.bfloat16

D_MODEL = 1024
BATCH = 32
SEQ = 2048
DEPTH = 2
N_META = 16
SSD_HEADS = 16
SSD_HEAD_DIM = 64
SSD_INNER = SSD_HEADS * SSD_HEAD_DIM
SSD_GROUPS = 4
SSD_HEADS_PER_GROUP = SSD_HEADS // SSD_GROUPS
SSD_STATE = 128
SSD_CONV = 4
S5_WIDTH = D_MODEL // 2
S5_GROUP = 16
S5_GROUPS = S5_WIDTH // S5_GROUP
S5_STATE = 64
S5_LANES = S5_GROUPS * S5_STATE
D_FF = 4 * D_MODEL
EPS = 1e-6
XBC_WIDTH = SSD_INNER + 2 * SSD_GROUPS * SSD_STATE

LANE = 128
SUBLANE = 8

SSD_T = 128
PAD_ROWS = SSD_T - N_META
LP = PAD_ROWS + N_META + SEQ
M_TOT = BATCH * LP
S5_SUB = 2
S5_TC = S5_SUB * SUBLANE
S5_ROWS = BATCH * S5_TC
S5_CTILE = 256
S5_USLABS = S5_WIDTH // LANE

PROJ_W = XBC_WIDTH + 2 * D_MODEL + SSD_INNER + S5_WIDTH
TN_PROJ = PROJ_W // 4
TM_TOK = 1024
TM_MERGE = 512
TF_MLP = 1024
VMEM_LIMIT = 48 * 1024 * 1024


def _sigmoid(x):
    return 0.5 * (jnp.tanh(0.5 * x) + 1.0)


def _rms_scale(x):
    return x * lax.rsqrt(jnp.mean(x * x, axis=-1, keepdims=True) + EPS)


def _bdot(a, b):
    return jnp.dot(a, b, preferred_element_type=F32)


def _inproj_kernel(h_ref, nw_ref, w_ref, wdt_ref, proj_ref, dt_ref, xn_ref):
    @pl.when(pl.program_id(1) == 0)
    def _():
        xn = (_rms_scale(h_ref[...]) * nw_ref[...]).astype(BF16)
        xn_ref[...] = xn
        dt_ref[...] = _bdot(xn, wdt_ref[...])

    proj_ref[...] = _bdot(xn_ref[...], w_ref[...])


def _inproj(h, norm_w, w_main, w_dt):
    return pl.pallas_call(
        _inproj_kernel,
        grid=(M_TOT // TM_TOK, PROJ_W // TN_PROJ),
        in_specs=[
            pl.BlockSpec((TM_TOK, D_MODEL), lambda i, j: (i, 0)),
            pl.BlockSpec((1, D_MODEL), lambda i, j: (0, 0)),
            pl.BlockSpec((D_MODEL, TN_PROJ), lambda i, j: (0, j)),
            pl.BlockSpec((D_MODEL, LANE), lambda i, j: (0, 0)),
        ],
        out_specs=[
            pl.BlockSpec((TM_TOK, TN_PROJ), lambda i, j: (i, j)),
            pl.BlockSpec((TM_TOK, LANE), lambda i, j: (i, 0)),
        ],
        out_shape=[
            jax.ShapeDtypeStruct((M_TOT, PROJ_W), F32),
            jax.ShapeDtypeStruct((M_TOT, LANE), F32),
        ],
        scratch_shapes=[pltpu.VMEM((TM_TOK, D_MODEL), BF16)],
        compiler_params=pltpu.CompilerParams(
            dimension_semantics=("arbitrary", "arbitrary"), vmem_limit_bytes=VMEM_LIMIT),
        name="inproj",
    )(h, norm_w, w_main, w_dt)


def _ssd_kernel(xbc_ref, z_ref, dt_ref, cw_ref, cb_ref, dtb_ref, alog_ref, dsk_ref, nw_ref,
                y_ref, ext_ref, act_ref, yacc_ref, st_ref):
    c = pl.program_id(1)
    t = SSD_T

    @pl.when(c == 0)
    def _():
        ext_ref[0:SUBLANE, :] = jnp.zeros((SUBLANE, XBC_WIDTH), F32)
        st_ref[...] = jnp.zeros_like(st_ref)

    ext_ref[SUBLANE:SUBLANE + t, :] = xbc_ref[...]
    for j in range(XBC_WIDTH // LANE):
        ls = slice(j * LANE, (j + 1) * LANE)
        acc = jnp.broadcast_to(cb_ref[:, ls], (t, LANE))
        for k in range(SSD_CONV):
            off = SUBLANE - (SSD_CONV - 1) + k
            acc = acc + cw_ref[k:k + 1, ls] * ext_ref[off:off + t, ls]
        act_ref[:, ls] = acc * _sigmoid(acc)
    ext_ref[0:SUBLANE, :] = ext_ref[t:t + SUBLANE, :]

    xdt = dt_ref[...] + dtb_ref[...]
    dt = jnp.maximum(xdt, 0.0) + jnp.log1p(jnp.exp(-jnp.abs(xdt)))
    row = c * t + lax.broadcasted_iota(jnp.int32, (t, LANE), 0)
    dt = jnp.where(row >= PAD_ROWS, dt, 0.0)
    a = -jnp.exp(alog_ref[...])
    r_i = lax.broadcasted_iota(jnp.int32, (t, t), 0)
    c_i = lax.broadcasted_iota(jnp.int32, (t, t), 1)
    causal = r_i >= c_i
    cs = jnp.dot(causal.astype(F32), dt * a, preferred_element_type=F32,
                 precision=lax.Precision.HIGHEST)
    cs_t = cs.T
    exp_cs = jnp.exp(cs)
    dec_end_t = jnp.exp(cs_t[:, t - 1:t] - cs_t)
    chunk_dec = jnp.exp(cs_t[:, t - 1:t])

    for g in range(SSD_GROUPS):
        b_g = act_ref[:, SSD_INNER + g * SSD_STATE:SSD_INNER + (g + 1) * SSD_STATE]
        c_lo = SSD_INNER + SSD_GROUPS * SSD_STATE + g * SSD_STATE
        c_g = act_ref[:, c_lo:c_lo + SSD_STATE]
        b_gt = b_g.T
        scores = _bdot(c_g.astype(BF16), b_gt.astype(BF16))
        for r in range(SSD_HEADS_PER_GROUP):
            h = g * SSD_HEADS_PER_GROUP + r
            hs = slice(h * SSD_HEAD_DIM, (h + 1) * SSD_HEAD_DIM)
            seg = cs[:, h:h + 1] - cs_t[h:h + 1, :]
            decay = jnp.exp(jnp.where(causal, seg, -jnp.inf))
            xs_h = act_ref[:, hs]
            xdt_h = (xs_h * dt[:, h:h + 1]).astype(BF16)
            y_diag = _bdot((scores * decay).astype(BF16), xdt_h)
            st_h = st_ref[h]
            y_off = _bdot((c_g * exp_cs[:, h:h + 1]).astype(BF16), st_h.astype(BF16))
            yacc_ref[:, hs] = y_diag + y_off
            b_dec_t = (b_gt * dec_end_t[h:h + 1, :]).astype(BF16)
            st_ref[h] = st_h * chunk_dec[h:h + 1, :] + _bdot(b_dec_t, xdt_h)

    gw = SSD_INNER // SSD_GROUPS
    for g in range(SSD_GROUPS):
        gs = slice(g * gw, (g + 1) * gw)
        z = z_ref[:, gs]
        y = (yacc_ref[:, gs] + dsk_ref[:, gs] * act_ref[:, gs]) * (z * _sigmoid(z))
        y_ref[:, gs] = _rms_scale(y) * nw_ref[:, gs]


def _ssd(proj3, dt3, conv_w, conv_b, dt_bias, a_log, d_full, norm_w):
    nc = LP // SSD_T
    z_blk = (XBC_WIDTH + 2 * D_MODEL) // SSD_INNER
    const = lambda b, c: (0, 0)
    return pl.pallas_call(
        _ssd_kernel,
        grid=(BATCH, nc),
        in_specs=[
            pl.BlockSpec((None, SSD_T, XBC_WIDTH), lambda b, c: (b, c, 0)),
            pl.BlockSpec((None, SSD_T, SSD_INNER), lambda b, c: (b, c, z_blk)),
            pl.BlockSpec((None, SSD_T, LANE), lambda b, c: (b, c, 0)),
            pl.BlockSpec((SSD_CONV, XBC_WIDTH), const),
            pl.BlockSpec((1, XBC_WIDTH), const),
            pl.BlockSpec((1, LANE), const),
            pl.BlockSpec((1, LANE), const),
            pl.BlockSpec((1, SSD_INNER), const),
            pl.BlockSpec((1, SSD_INNER), const),
        ],
        out_specs=pl.BlockSpec((None, SSD_T, SSD_INNER), lambda b, c: (b, c, 0)),
        out_shape=jax.ShapeDtypeStruct((BATCH, LP, SSD_INNER), F32),
        scratch_shapes=[
            pltpu.VMEM((SSD_T + SUBLANE, XBC_WIDTH), F32),
            pltpu.VMEM((SSD_T, XBC_WIDTH), F32),
            pltpu.VMEM((SSD_T, SSD_INNER), F32),
            pltpu.VMEM((SSD_HEADS, SSD_STATE, SSD_HEAD_DIM), F32),
        ],
        compiler_params=pltpu.CompilerParams(
            dimension_semantics=("arbitrary", "arbitrary"), vmem_limit_bytes=VMEM_LIMIT),
        name="ssd",
    )(proj3, proj3, dt3, conv_w, conv_b, dt_bias, a_log, d_full, norm_w)


def _s5_disc_kernel(are_ref, aim_ref, ls_ref, bre_ref, bim_ref, abre_ref, abim_ref, bbre_ref, bbim_ref):
    step = jnp.exp(ls_ref[...])
    lam_re = are_ref[...]
    lam_im = aim_ref[...]
    mag = jnp.exp(lam_re * step)
    ab_re = mag * jnp.cos(lam_im * step)
    ab_im = mag * jnp.sin(lam_im * step)
    den = lam_re * lam_re + lam_im * lam_im
    nr = ab_re - 1.0
    f_re = (nr * lam_re + ab_im * lam_im) / den
    f_im = (ab_im * lam_re - nr * lam_im) / den
    br = bre_ref[...]
    bi = bim_ref[...]
    abre_ref[...] = ab_re
    abim_ref[...] = ab_im
    bbre_ref[...] = f_re * br - f_im * bi
    bbim_ref[...] = f_re * bi + f_im * br


def _s5_disc(a_re, a_im, log_step, b_re_t, b_im_t):
    sds = jax.ShapeDtypeStruct
    return pl.pallas_call(
        _s5_disc_kernel,
        out_shape=[sds(a_re.shape, F32), sds(a_re.shape, F32), sds(b_re_t.shape, F32), sds(b_re_t.shape, F32)],
        name="s5_disc",
    )(a_re, a_im, log_step, b_re_t, b_im_t)


def _s5_kernel(u_ref, bre_ref, bim_ref, are_ref, aim_ref, cre_ref, cim_ref, y_ref,
               uslab_ref, yslab_ref, bu_ref, h_ref):
    @pl.when(pl.program_id(0) == 0)
    def _():
        h_ref[...] = jnp.zeros_like(h_ref)

    blk_rows = BATCH * SUBLANE

    for q in range(S5_SUB):
        uq = u_ref[:, q * SUBLANE:(q + 1) * SUBLANE, :].reshape(blk_rows, S5_WIDTH)
        for k in range(S5_USLABS):
            uslab_ref[q * S5_USLABS + k] = uq[:, k * LANE:(k + 1) * LANE]
    steps = []
    for t in range(S5_TC):
        q, tl = divmod(t, SUBLANE)
        rows = pl.ds(tl, BATCH, stride=SUBLANE)
        steps.append(jnp.concatenate(
            [uslab_ref[q * S5_USLABS + k, rows, :] for k in range(S5_USLABS)], axis=1))
    u = jnp.concatenate(steps, axis=0).astype(BF16)

    n_half = bre_ref.shape[0]
    kw = S5_WIDTH // n_half
    nw = S5_LANES // n_half
    for k in range(n_half):
        uk = u[:, k * kw:(k + 1) * kw]
        bu_ref[:, k * nw:(k + 1) * nw] = _bdot(uk, bre_ref[k])
        bu_ref[:, S5_LANES + k * nw:S5_LANES + (k + 1) * nw] = _bdot(uk, bim_ref[k])

    for s in range(S5_LANES // LANE):
        ls = slice(s * LANE, (s + 1) * LANE)
        li = slice(S5_LANES + s * LANE, S5_LANES + (s + 1) * LANE)
        a_r = are_ref[:, ls]
        a_i = aim_ref[:, ls]
        h_r = h_ref[:, ls]
        h_i = h_ref[:, li]
        for t in range(S5_TC):
            rows = slice(t * BATCH, (t + 1) * BATCH)
            n_r = a_r * h_r - a_i * h_i + bu_ref[rows, ls]
            n_i = a_r * h_i + a_i * h_r + bu_ref[rows, li]
            h_r, h_i = n_r, n_i
            bu_ref[rows, ls] = h_r
            bu_ref[rows, li] = h_i
        h_ref[:, ls] = h_r
        h_ref[:, li] = h_i

    pieces = []
    for j in range(cre_ref.shape[0]):
        ks = slice(j * S5_CTILE, (j + 1) * S5_CTILE)
        ki = slice(S5_LANES + j * S5_CTILE, S5_LANES + (j + 1) * S5_CTILE)
        pieces.append(_bdot(bu_ref[:, ks].astype(BF16), cre_ref[j])
                      - _bdot(bu_ref[:, ki].astype(BF16), cim_ref[j]))
    y = jnp.concatenate(pieces, axis=1)

    for t in range(S5_TC):
        q, tl = divmod(t, SUBLANE)
        rows = pl.ds(tl, BATCH, stride=SUBLANE)
        for k in range(S5_USLABS):
            yslab_ref[q * S5_USLABS + k, rows, :] = y[t * BATCH:(t + 1) * BATCH, k * LANE:(k + 1) * LANE]
    for q in range(S5_SUB):
        yq = jnp.concatenate([yslab_ref[q * S5_USLABS + k] for k in range(S5_USLABS)], axis=1)
        y_ref[:, q * SUBLANE:(q + 1) * SUBLANE, :] = yq.reshape(BATCH, SUBLANE, S5_WIDTH)


def _s5(proj3, bre_bd, bim_bd, ab_re, ab_im, cre_bd, cim_bd):
    u_blk = (PROJ_W - S5_WIDTH) // S5_WIDTH
    c2 = lambda i: (0, 0)
    c3 = lambda i: (0, 0, 0)
    return pl.pallas_call(
        _s5_kernel,
        grid=(LP // S5_TC,),
        in_specs=[
            pl.BlockSpec((BATCH, S5_TC, S5_WIDTH), lambda i: (0, i, u_blk)),
            pl.BlockSpec(bre_bd.shape, c3),
            pl.BlockSpec(bim_bd.shape, c3),
            pl.BlockSpec((1, S5_LANES), c2),
            pl.BlockSpec((1, S5_LANES), c2),
            pl.BlockSpec(cre_bd.shape, c3),
            pl.BlockSpec(cim_bd.shape, c3),
        ],
        out_specs=pl.BlockSpec((BATCH, S5_TC, S5_WIDTH), lambda i: (0, i, 0)),
        out_shape=jax.ShapeDtypeStruct((BATCH, LP, S5_WIDTH), F32),
        scratch_shapes=[
            pltpu.VMEM((S5_SUB * S5_USLABS, BATCH * SUBLANE, LANE), F32),
            pltpu.VMEM((S5_SUB * S5_USLABS, BATCH * SUBLANE, LANE), F32),
            pltpu.VMEM((S5_ROWS, 2 * S5_LANES), F32),
            pltpu.VMEM((BATCH, 2 * S5_LANES), F32),
        ],
        compiler_params=pltpu.CompilerParams(
            dimension_semantics=("arbitrary",), vmem_limit_bytes=VMEM_LIMIT),
        name="s5",
    )(proj3, bre_bd, bim_bd, ab_re, ab_im, cre_bd, cim_bd)


def _merge_kernel(g_ref, ya_ref, ys_ref, u_ref, h_ref, sd_ref, wglu_ref, wout_ref, o_ref):
    ys = ys_ref[...] + sd_ref[...] * u_ref[...]
    cdf = 0.5 * (1.0 + jnp.tanh(math.sqrt(2.0 / math.pi) * (ys + 0.044715 * (ys * ys * ys))))
    vg = _bdot((ys * cdf).astype(BF16), wglu_ref[...])
    yb = vg[:, :D_MODEL] * _sigmoid(vg[:, D_MODEL:])
    m = _sigmoid(g_ref[:, :D_MODEL]) * ya_ref[...] + _sigmoid(g_ref[:, D_MODEL:]) * yb
    o_ref[...] = h_ref[...] + _bdot(m.astype(BF16), wout_ref[...])


def _merge(proj, y_a, y_s5, h, s5_d, w_glu, w_out):
    tm = TM_MERGE
    g_blk = XBC_WIDTH // (2 * D_MODEL)
    u_blk = (PROJ_W - S5_WIDTH) // S5_WIDTH
    const = lambda i: (0, 0)
    return pl.pallas_call(
        _merge_kernel,
        grid=(M_TOT // tm,),
        in_specs=[
            pl.BlockSpec((tm, 2 * D_MODEL), lambda i: (i, g_blk)),
            pl.BlockSpec((tm, SSD_INNER), lambda i: (i, 0)),
            pl.BlockSpec((tm, S5_WIDTH), lambda i: (i, 0)),
            pl.BlockSpec((tm, S5_WIDTH), lambda i: (i, u_blk)),
            pl.BlockSpec((tm, D_MODEL), lambda i: (i, 0)),
            pl.BlockSpec((1, S5_WIDTH), const),
            pl.BlockSpec((S5_WIDTH, 2 * D_MODEL), const),
            pl.BlockSpec((D_MODEL, D_MODEL), const),
        ],
        out_specs=pl.BlockSpec((tm, D_MODEL), lambda i: (i, 0)),
        out_shape=jax.ShapeDtypeStruct((M_TOT, D_MODEL), F32),
        compiler_params=pltpu.CompilerParams(
            dimension_semantics=("arbitrary",), vmem_limit_bytes=VMEM_LIMIT),
        name="merge",
    )(proj, y_a, y_s5, proj, h, s5_d, w_glu, w_out)


def _mlp_kernel(h_ref, nw_ref, w1_ref, w2_ref, o_ref, xn_ref):
    @pl.when(pl.program_id(1) == 0)
    def _():
        x = h_ref[...]
        xn_ref[...] = (_rms_scale(x) * nw_ref[...]).astype(BF16)
        o_ref[...] = x

    hid = jnp.square(jnp.maximum(_bdot(xn_ref[...], w1_ref[...]), 0.0))
    o_ref[...] += _bdot(hid.astype(BF16), w2_ref[...])


def _mlp(h, norm_w, w1, w2):
    return pl.pallas_call(
        _mlp_kernel,
        grid=(M_TOT // TM_TOK, D_FF // TF_MLP),
        in_specs=[
            pl.BlockSpec((TM_TOK, D_MODEL), lambda i, f: (i, 0)),
            pl.BlockSpec((1, D_MODEL), lambda i, f: (0, 0)),
            pl.BlockSpec((D_MODEL, TF_MLP), lambda i, f: (0, f)),
            pl.BlockSpec((TF_MLP, D_MODEL), lambda i, f: (f, 0)),
        ],
        out_specs=pl.BlockSpec((TM_TOK, D_MODEL), lambda i, f: (i, 0)),
        out_shape=jax.ShapeDtypeStruct((M_TOT, D_MODEL), F32),
        scratch_shapes=[pltpu.VMEM((TM_TOK, D_MODEL), BF16)],
        compiler_params=pltpu.CompilerParams(
            dimension_semantics=("arbitrary", "arbitrary"), vmem_limit_bytes=VMEM_LIMIT),
        name="mlp",
    )(h, norm_w, w1, w2)


def _final_kernel(h_ref, nw_ref, o_ref):
    o_ref[...] = _rms_scale(h_ref[LP - SEQ:, :]) * nw_ref[...]


def _final(h3, norm_w):
    return pl.pallas_call(
        _final_kernel,
        grid=(BATCH,),
        in_specs=[
            pl.BlockSpec((None, LP, D_MODEL), lambda b: (b, 0, 0)),
            pl.BlockSpec((1, D_MODEL), lambda b: (0, 0)),
        ],
        out_specs=pl.BlockSpec((None, SEQ, D_MODEL), lambda b: (b, 0, 0)),
        out_shape=jax.ShapeDtypeStruct((BATCH, SEQ, D_MODEL), F32),
        compiler_params=pltpu.CompilerParams(
            dimension_semantics=("arbitrary",), vmem_limit_bytes=VMEM_LIMIT),
        name="final_norm",
    )(h3, norm_w)


def _pad_lanes(v):
    return jnp.pad(v, (0, LANE - v.shape[0])).reshape(1, LANE)


def _block_diag(blocks):
    n, r, c = blocks.shape
    eye = jnp.eye(n, dtype=blocks.dtype)
    return (blocks[:, :, None, :] * eye[:, None, :, None]).reshape(n * r, n * c)


def kernel(x, meta_tokens, w_in, conv_w, conv_b, dt_bias, ssd_a_log, ssd_d, ssd_norm_w, s5_a_re, s5_a_im, s5_log_step, s5_b_re, s5_b_im, s5_c_re, s5_c_im, s5_d, w_glu, w_out, norm_mix_w, norm_mlp_w, w_ff_in, w_ff_out, final_norm_w):
    bsz = x.shape[0]
    meta = jnp.broadcast_to(meta_tokens[None].astype(x.dtype), (bsz, N_META, D_MODEL))
    pad = jnp.zeros((bsz, PAD_ROWS, D_MODEL), x.dtype)
    h = jnp.concatenate([pad, meta, x], axis=1).reshape(M_TOT, D_MODEL)

    o_xbc = SSD_INNER
    o_dt = o_xbc + XBC_WIDTH
    o_u = o_dt + SSD_HEADS
    o_g = o_u + S5_WIDTH
    n_half = 2
    gph = S5_GROUPS // n_half

    for i in range(DEPTH):
        wi = w_in[i]
        w_main = jnp.concatenate(
            [wi[:, o_xbc:o_dt], wi[:, o_g:], wi[:, :o_xbc], wi[:, o_u:o_g]], axis=1).astype(BF16)
        w_dt = jnp.pad(wi[:, o_dt:o_u], ((0, 0), (0, LANE - SSD_HEADS))).astype(BF16)
        proj, dt_raw = _inproj(h, norm_mix_w[i].reshape(1, D_MODEL), w_main, w_dt)
        proj3 = proj.reshape(BATCH, LP, PROJ_W)

        y_a = _ssd(
            proj3, dt_raw.reshape(BATCH, LP, LANE), conv_w[i], conv_b[i].reshape(1, XBC_WIDTH),
            _pad_lanes(dt_bias[i]), _pad_lanes(ssd_a_log[i]),
            jnp.repeat(ssd_d[i], SSD_HEAD_DIM).reshape(1, SSD_INNER),
            ssd_norm_w[i].reshape(1, SSD_INNER))

        ab_re, ab_im, bb_re_t, bb_im_t = _s5_disc(
            s5_a_re[i].reshape(S5_GROUPS, 1, S5_STATE), s5_a_im[i].reshape(S5_GROUPS, 1, S5_STATE),
            s5_log_step[i].reshape(S5_GROUPS, 1, 1),
            jnp.swapaxes(s5_b_re[i], 1, 2), jnp.swapaxes(s5_b_im[i], 1, 2))
        bd_b = lambda bt: jnp.stack(
            [_block_diag(bt[k * gph:(k + 1) * gph]) for k in range(n_half)]).astype(BF16)
        gpc = S5_CTILE // S5_STATE
        bd_c = lambda cm: jnp.stack(
            [_block_diag(jnp.swapaxes(cm, 1, 2)[j * gpc:(j + 1) * gpc])
             for j in range(S5_GROUPS // gpc)]).astype(BF16)
        y_s5 = _s5(
            proj3, bd_b(bb_re_t), bd_b(bb_im_t),
            ab_re.reshape(1, S5_LANES), ab_im.reshape(1, S5_LANES),
            bd_c(s5_c_re[i]), bd_c(s5_c_im[i]))

        h = _merge(proj, y_a.reshape(M_TOT, SSD_INNER), y_s5.reshape(M_TOT, S5_WIDTH), h,
                   s5_d[i].reshape(1, S5_WIDTH), w_glu[i].astype(BF16), w_out[i].astype(BF16))
        h = _mlp(h, norm_mlp_w[i].reshape(1, D_MODEL), w_ff_in[i].astype(BF16), w_ff_out[i].astype(BF16))

    return _final(h.reshape(BATCH, LP, D_MODEL), final_norm_w.reshape(1, D_MODEL))
```

```python
import functools
import math

import jax
import jax.numpy as jnp
from jax import lax
from jax.experimental import pallas as pl
from jax.experimental.pallas import tpu as pltpu

F32 = jnp.float32
BF16 = jnp.bfloat16

D_MODEL = 1024
BATCH = 32
SEQ = 2048
DEPTH = 2
N_META = 16
SSD_HEADS = 16
SSD_HEAD_DIM = 64
SSD_INNER = SSD_HEADS * SSD_HEAD_DIM
SSD_GROUPS = 4
SSD_HEADS_PER_GROUP = SSD_HEADS // SSD_GROUPS
SSD_STATE = 128
SSD_CONV = 4
S5_WIDTH = D_MODEL // 2
S5_GROUP = 16
S5_GROUPS = S5_WIDTH // S5_GROUP
S5_STATE = 64
S5_LANES = S5_GROUPS * S5_STATE
D_FF = 4 * D_MODEL
EPS = 1e-6
XBC_WIDTH = SSD_INNER + 2 * SSD_GROUPS * SSD_STATE

LANE = 128
SUBLANE = 8

SSD_T = 128
PAD_ROWS = SSD_T - N_META
LP = PAD_ROWS + N_META + SEQ
M_TOT = BATCH * LP
S5_SUB = 2
S5_TC = S5_SUB * SUBLANE
S5_ROWS = BATCH * S5_TC
S5_CTILE = 256
S5_USLABS = S5_WIDTH // LANE

OTH_W = 2 * D_MODEL + SSD_INNER + S5_WIDTH
NJ_PROJ = 4
TN_ACT = XBC_WIDTH // NJ_PROJ
TN_OTH = OTH_W // NJ_PROJ
TM_PROJ = LP // 2
CONV_ROWS = 64
TM_TOK = 1024
TM_MERGE = 512
TF_MLP = 1024
VMEM_LIMIT = 48 * 1024 * 1024


def _sigmoid(x):
    return 0.5 * (jnp.tanh(0.5 * x) + 1.0)


def _rms_scale(x):
    return x * lax.rsqrt(jnp.mean(x * x, axis=-1, keepdims=True) + EPS)


def _bdot(a, b):
    return jnp.dot(a, b, preferred_element_type=F32)


def _inproj_kernel(h_ref, nw_ref, wx_ref, wo_ref, wdt_ref, cw_ref, cb_ref,
                   act_ref, oth_ref, dt_ref, xn_ref, ext_ref, carry_ref):
    i = pl.program_id(1)
    j = pl.program_id(2)

    @pl.when(j == 0)
    def _():
        xn = (_rms_scale(h_ref[...]) * nw_ref[...]).astype(BF16)
        xn_ref[...] = xn
        dt_ref[...] = _bdot(xn, wdt_ref[...])

    xn = xn_ref[...]
    ext_ref[0:SUBLANE, :] = jnp.where(i == 0, 0.0, carry_ref[j])
    ext_ref[SUBLANE:SUBLANE + TM_PROJ, :] = _bdot(xn, wx_ref[...])
    carry_ref[j] = ext_ref[TM_PROJ:TM_PROJ + SUBLANE, :]
    oth_ref[...] = _bdot(xn, wo_ref[...]).astype(BF16)
    for lb in range(TN_ACT // LANE):
        ls = slice(lb * LANE, (lb + 1) * LANE)
        for rc in range(TM_PROJ // CONV_ROWS):
            r0 = rc * CONV_ROWS
            xe = ext_ref[r0:r0 + SUBLANE + CONV_ROWS, ls]
            acc = cb_ref[:, ls] + cw_ref[SSD_CONV - 1:SSD_CONV, ls] * xe[SUBLANE:, :]
            for k in range(SSD_CONV - 1):
                shifted = pltpu.roll(xe, SSD_CONV - 1 - k, 0)[SUBLANE:, :]
                acc = acc + cw_ref[k:k + 1, ls] * shifted
            act_ref[r0:r0 + CONV_ROWS, ls] = (acc * _sigmoid(acc)).astype(BF16)


def _inproj(h3, norm_w, w_x, w_o, w_dt, conv_w, conv_b):
    c2 = lambda b, i, j: (0, 0)
    return pl.pallas_call(
        _inproj_kernel,
        grid=(BATCH, LP // TM_PROJ, NJ_PROJ),
        in_specs=[
            pl.BlockSpec((None, TM_PROJ, D_MODEL), lambda b, i, j: (b, i, 0)),
            pl.BlockSpec((1, D_MODEL), c2),
            pl.BlockSpec((D_MODEL, TN_ACT), lambda b, i, j: (0, j)),
            pl.BlockSpec((D_MODEL, TN_OTH), lambda b, i, j: (0, j)),
            pl.BlockSpec((D_MODEL, LANE), c2),
            pl.BlockSpec((SSD_CONV, TN_ACT), lambda b, i, j: (0, j)),
            pl.BlockSpec((1, TN_ACT), lambda b, i, j: (0, j)),
        ],
        out_specs=[
            pl.BlockSpec((None, TM_PROJ, TN_ACT), lambda b, i, j: (b, i, j)),
            pl.BlockSpec((None, TM_PROJ, TN_OTH), lambda b, i, j: (b, i, j)),
            pl.BlockSpec((None, TM_PROJ, LANE), lambda b, i, j: (b, i, 0)),
        ],
        out_shape=[
            jax.ShapeDtypeStruct((BATCH, LP, XBC_WIDTH), BF16),
            jax.ShapeDtypeStruct((BATCH, LP, OTH_W), BF16),
            jax.ShapeDtypeStruct((BATCH, LP, LANE), F32),
        ],
        scratch_shapes=[
            pltpu.VMEM((TM_PROJ, D_MODEL), BF16),
            pltpu.VMEM((TM_PROJ + SUBLANE, TN_ACT), F32),
            pltpu.VMEM((NJ_PROJ, SUBLANE, TN_ACT), F32),
        ],
        compiler_params=pltpu.CompilerParams(
            dimension_semantics=("arbitrary", "arbitrary", "arbitrary"), vmem_limit_bytes=VMEM_LIMIT),
        name="inproj",
    )(h3, norm_w, w_x, w_o, w_dt, conv_w, conv_b)


def _split_bf16(x, n):
    parts = []
    for _ in range(n):
        p = x.astype(BF16)
        parts.append(p)
        x = x - p.astype(F32)
    return jnp.concatenate(parts, axis=1)


def _ssd_kernel(act_ref, z_ref, dt_ref, dtb_ref, alog_ref, exp_ref, hmask_ref, dsk_ref, nw_ref,
                y_ref, st_ref):
    c = pl.program_id(1)
    t = SSD_T
    gw = SSD_INNER // SSD_GROUPS

    @pl.when(c == 0)
    def _():
        st_ref[...] = jnp.zeros_like(st_ref)

    xdt = dt_ref[...] + dtb_ref[...]
    dt = jnp.maximum(xdt, 0.0) + jnp.log1p(jnp.exp(-jnp.abs(xdt)))
    row = c * t + lax.broadcasted_iota(jnp.int32, (t, LANE), 0)
    dt = jnp.where(row >= PAD_ROWS, dt, 0.0)
    a = -jnp.exp(alog_ref[...])
    r_i = lax.broadcasted_iota(jnp.int32, (t, t), 0)
    c_i = lax.broadcasted_iota(jnp.int32, (t, t), 1)
    causal = r_i >= c_i
    cs = jnp.dot(causal.astype(F32), dt * a, preferred_element_type=F32,
                 precision=lax.Precision.HIGHEST)
    cs_t = cs.T

    dt_ch = _bdot(_split_bf16(dt, 2), exp_ref[0:2 * LANE, :])
    cs_ch = _bdot(_split_bf16(cs, 3), exp_ref[...])
    cs_end = cs_ch[t - 1:t, :]
    xs = act_ref[:, :SSD_INNER].astype(F32)
    xdt = xs * dt_ch
    xdt_b = xdt.astype(BF16)
    xdt_end_b = (xdt * jnp.exp(cs_end - cs_ch)).astype(BF16)

    for g in range(SSD_GROUPS):
        gs = slice(g * gw, (g + 1) * gw)
        b_lo = SSD_INNER + g * SSD_STATE
        c_lo = SSD_INNER + SSD_GROUPS * SSD_STATE + g * SSD_STATE
        b_gt = act_ref[:, b_lo:b_lo + SSD_STATE].astype(F32).T.astype(BF16)
        c_g = act_ref[:, c_lo:c_lo + SSD_STATE]
        scores = _bdot(c_g, b_gt)
        m_parts = []
        for r in range(SSD_HEADS_PER_GROUP):
            h = g * SSD_HEADS_PER_GROUP + r
            seg = cs[:, h:h + 1] - cs_t[h:h + 1, :]
            decay = jnp.exp(jnp.where(causal, seg, -jnp.inf))
            m_parts.append((scores * decay).astype(BF16))
        x_bd = jnp.concatenate([xdt_b[:, gs]] * SSD_HEADS_PER_GROUP, axis=0) * hmask_ref[...]
        y = _bdot(jnp.concatenate(m_parts, axis=1), x_bd)
        st = st_ref[g]
        y = y + _bdot(c_g, st.astype(BF16)) * jnp.exp(cs_ch[:, gs])
        st_ref[g] = st * jnp.exp(cs_end[:, gs]) + _bdot(b_gt, xdt_end_b[:, gs])

        z = z_ref[:, gs].astype(F32)
        y = (y + dsk_ref[:, gs] * xs[:, gs]) * (z * _sigmoid(z))
        y_ref[:, gs] = (_rms_scale(y) * nw_ref[:, gs]).astype(BF16)


def _ssd(act3, oth3, dt3, dt_bias, a_log, d_full, norm_w):
    nc = LP // SSD_T
    z_blk = 2 * D_MODEL // SSD_INNER
    gw = SSD_INNER // SSD_GROUPS
    const = lambda b, c: (0, 0)
    head_of_ch = jnp.arange(SSD_INNER) // SSD_HEAD_DIM
    expand = (jnp.arange(LANE)[:, None] == head_of_ch[None, :]).astype(BF16)
    expand3 = jnp.tile(expand, (3, 1))
    head_of_row = jnp.arange(SSD_HEADS_PER_GROUP * SSD_T) // SSD_T
    hmask = (head_of_row[:, None] == head_of_ch[None, :gw]).astype(BF16)
    return pl.pallas_call(
        _ssd_kernel,
        grid=(BATCH, nc),
        in_specs=[
            pl.BlockSpec((None, SSD_T, XBC_WIDTH), lambda b, c: (b, c, 0)),
            pl.BlockSpec((None, SSD_T, SSD_INNER), lambda b, c: (b, c, z_blk)),
            pl.BlockSpec((None, SSD_T, LANE), lambda b, c: (b, c, 0)),
            pl.BlockSpec((1, LANE), const),
            pl.BlockSpec((1, LANE), const),
            pl.BlockSpec((3 * LANE, SSD_INNER), const),
            pl.BlockSpec((SSD_HEADS_PER_GROUP * SSD_T, gw), const),
            pl.BlockSpec((1, SSD_INNER), const),
            pl.BlockSpec((1, SSD_INNER), const),
        ],
        out_specs=pl.BlockSpec((None, SSD_T, SSD_INNER), lambda b, c: (b, c, 0)),
        out_shape=jax.ShapeDtypeStruct((BATCH, LP, SSD_INNER), BF16),
        scratch_shapes=[pltpu.VMEM((SSD_GROUPS, SSD_STATE, gw), F32)],
        compiler_params=pltpu.CompilerParams(
            dimension_semantics=("arbitrary", "arbitrary"), vmem_limit_bytes=VMEM_LIMIT),
        name="ssd",
    )(act3, oth3, dt3, dt_bias, a_log, expand3, hmask, d_full, norm_w)


def _s5_disc_kernel(are_ref, aim_ref, ls_ref, bre_ref, bim_ref, abre_ref, abim_ref, bbre_ref, bbim_ref):
    step = jnp.exp(ls_ref[...])
    lam_re = are_ref[...]
    lam_im = aim_ref[...]
    mag = jnp.exp(lam_re * step)
    ab_re = mag * jnp.cos(lam_im * step)
    ab_im = mag * jnp.sin(lam_im * step)
    den = lam_re * lam_re + lam_im * lam_im
    nr = ab_re - 1.0
    f_re = (nr * lam_re + ab_im * lam_im) / den
    f_im = (ab_im * lam_re - nr * lam_im) / den
    br = bre_ref[...]
    bi = bim_ref[...]
    abre_ref[...] = ab_re
    abim_ref[...] = ab_im
    bbre_ref[...] = f_re * br - f_im * bi
    bbim_ref[...] = f_re * bi + f_im * br


def _s5_disc(a_re, a_im, log_step, b_re_t, b_im_t):
    sds = jax.ShapeDtypeStruct
    return pl.pallas_call(
        _s5_disc_kernel,
        out_shape=[sds(a_re.shape, F32), sds(a_re.shape, F32), sds(b_re_t.shape, F32), sds(b_re_t.shape, F32)],
        name="s5_disc",
    )(a_re, a_im, log_step, b_re_t, b_im_t)


def _s5_kernel(u_ref, bre_ref, bim_ref, are_ref, aim_ref, cre_ref, cim_ref, y_ref,
               uslab_ref, yslab_ref, bu_ref, h_ref):
    @pl.when(pl.program_id(0) == 0)
    def _():
        h_ref[...] = jnp.zeros_like(h_ref)

    blk_rows = BATCH * SUBLANE

    u32 = u_ref[...].astype(F32)
    for q in range(S5_SUB):
        uq = u32[:, q * SUBLANE:(q + 1) * SUBLANE, :].reshape(blk_rows, S5_WIDTH)
        for k in range(S5_USLABS):
            uslab_ref[q * S5_USLABS + k] = uq[:, k * LANE:(k + 1) * LANE]
    steps = []
    for t in range(S5_TC):
        q, tl = divmod(t, SUBLANE)
        rows = pl.ds(tl, BATCH, stride=SUBLANE)
        steps.append(jnp.concatenate(
            [uslab_ref[q * S5_USLABS + k, rows, :] for k in range(S5_USLABS)], axis=1))
    u = jnp.concatenate(steps, axis=0).astype(BF16)

    n_half = bre_ref.shape[0]
    kw = S5_WIDTH // n_half
    nw = S5_LANES // n_half
    for k in range(n_half):
        uk = u[:, k * kw:(k + 1) * kw]
        bu_ref[:, k * nw:(k + 1) * nw] = _bdot(uk, bre_ref[k])
        bu_ref[:, S5_LANES + k * nw:S5_LANES + (k + 1) * nw] = _bdot(uk, bim_ref[k])

    for s in range(S5_LANES // LANE):
        ls = slice(s * LANE, (s + 1) * LANE)
        li = slice(S5_LANES + s * LANE, S5_LANES + (s + 1) * LANE)
        a_r = are_ref[:, ls]
        a_i = aim_ref[:, ls]
        h_r = h_ref[:, ls]
        h_i = h_ref[:, li]
        for t in range(S5_TC):
            rows = slice(t * BATCH, (t + 1) * BATCH)
            n_r = a_r * h_r - a_i * h_i + bu_ref[rows, ls]
            n_i = a_r * h_i + a_i * h_r + bu_ref[rows, li]
            h_r, h_i = n_r, n_i
            bu_ref[rows, ls] = h_r
            bu_ref[rows, li] = h_i
        h_ref[:, ls] = h_r
        h_ref[:, li] = h_i

    pieces = []
    for j in range(cre_ref.shape[0]):
        ks = slice(j * S5_CTILE, (j + 1) * S5_CTILE)
        ki = slice(S5_LANES + j * S5_CTILE, S5_LANES + (j + 1) * S5_CTILE)
        pieces.append(_bdot(bu_ref[:, ks].astype(BF16), cre_ref[j])
                      - _bdot(bu_ref[:, ki].astype(BF16), cim_ref[j]))
    y = jnp.concatenate(pieces, axis=1)

    for t in range(S5_TC):
        q, tl = divmod(t, SUBLANE)
        rows = pl.ds(tl, BATCH, stride=SUBLANE)
        for k in range(S5_USLABS):
            yslab_ref[q * S5_USLABS + k, rows, :] = y[t * BATCH:(t + 1) * BATCH, k * LANE:(k + 1) * LANE]
    halves = []
    for q in range(S5_SUB):
        yq = jnp.concatenate([yslab_ref[q * S5_USLABS + k] for k in range(S5_USLABS)], axis=1)
        halves.append(yq.reshape(BATCH, SUBLANE, S5_WIDTH))
    y_ref[...] = jnp.concatenate(halves, axis=1).astype(BF16)


def _s5(oth3, bre_bd, bim_bd, ab_re, ab_im, cre_bd, cim_bd):
    u_blk = (OTH_W - S5_WIDTH) // S5_WIDTH
    c2 = lambda i: (0, 0)
    c3 = lambda i: (0, 0, 0)
    return pl.pallas_call(
        _s5_kernel,
        grid=(LP // S5_TC,),
        in_specs=[
            pl.BlockSpec((BATCH, S5_TC, S5_WIDTH), lambda i: (0, i, u_blk)),
            pl.BlockSpec(bre_bd.shape, c3),
            pl.BlockSpec(bim_bd.shape, c3),
            pl.BlockSpec((1, S5_LANES), c2),
            pl.BlockSpec((1, S5_LANES), c2),
            pl.BlockSpec(cre_bd.shape, c3),
            pl.BlockSpec(cim_bd.shape, c3),
        ],
        out_specs=pl.BlockSpec((BATCH, S5_TC, S5_WIDTH), lambda i: (0, i, 0)),
        out_shape=jax.ShapeDtypeStruct((BATCH, LP, S5_WIDTH), BF16),
        scratch_shapes=[
            pltpu.VMEM((S5_SUB * S5_USLABS, BATCH * SUBLANE, LANE), F32),
            pltpu.VMEM((S5_SUB * S5_USLABS, BATCH * SUBLANE, LANE), F32),
            pltpu.VMEM((S5_ROWS, 2 * S5_LANES), F32),
            pltpu.VMEM((BATCH, 2 * S5_LANES), F32),
        ],
        compiler_params=pltpu.CompilerParams(
            dimension_semantics=("arbitrary",), vmem_limit_bytes=VMEM_LIMIT),
        name="s5",
    )(oth3, bre_bd, bim_bd, ab_re, ab_im, cre_bd, cim_bd)


def _merge_kernel(g_ref, ya_ref, ys_ref, u_ref, h_ref, sd_ref, wglu_ref, wout_ref, o_ref):
    ys = ys_ref[...].astype(F32) + sd_ref[...] * u_ref[...].astype(F32)
    cdf = 0.5 * (1.0 + jnp.tanh(math.sqrt(2.0 / math.pi) * (ys + 0.044715 * (ys * ys * ys))))
    vg = _bdot((ys * cdf).astype(BF16), wglu_ref[...])
    yb = vg[:, :D_MODEL] * _sigmoid(vg[:, D_MODEL:])
    g = g_ref[...].astype(F32)
    m = _sigmoid(g[:, :D_MODEL]) * ya_ref[...].astype(F32) + _sigmoid(g[:, D_MODEL:]) * yb
    o_ref[...] = h_ref[...] + _bdot(m.astype(BF16), wout_ref[...])


def _merge(oth, y_a, y_s5, h, s5_d, w_glu, w_out):
    tm = TM_MERGE
    u_blk = (OTH_W - S5_WIDTH) // S5_WIDTH
    const = lambda i: (0, 0)
    return pl.pallas_call(
        _merge_kernel,
        grid=(M_TOT // tm,),
        in_specs=[
            pl.BlockSpec((tm, 2 * D_MODEL), lambda i: (i, 0)),
            pl.BlockSpec((tm, SSD_INNER), lambda i: (i, 0)),
            pl.BlockSpec((tm, S5_WIDTH), lambda i: (i, 0)),
            pl.BlockSpec((tm, S5_WIDTH), lambda i: (i, u_blk)),
            pl.BlockSpec((tm, D_MODEL), lambda i: (i, 0)),
            pl.BlockSpec((1, S5_WIDTH), const),
            pl.BlockSpec((S5_WIDTH, 2 * D_MODEL), const),
            pl.BlockSpec((D_MODEL, D_MODEL), const),
        ],
        out_specs=pl.BlockSpec((tm, D_MODEL), lambda i: (i, 0)),
        out_shape=jax.ShapeDtypeStruct((M_TOT, D_MODEL), F32),
        compiler_params=pltpu.CompilerParams(
            dimension_semantics=("arbitrary",), vmem_limit_bytes=VMEM_LIMIT),
        name="merge",
    )(oth, y_a, y_s5, oth, h, s5_d, w_glu, w_out)


def _mlp_kernel(h_ref, nw_ref, w1_ref, w2_ref, o_ref, xn_ref):
    @pl.when(pl.program_id(1) == 0)
    def _():
        x = h_ref[...]
        xn_ref[...] = (_rms_scale(x) * nw_ref[...]).astype(BF16)
        o_ref[...] = x

    hid = jnp.square(jnp.maximum(_bdot(xn_ref[...], w1_ref[...]), 0.0))
    o_ref[...] += _bdot(hid.astype(BF16), w2_ref[...])


def _mlp(h, norm_w, w1, w2):
    return pl.pallas_call(
        _mlp_kernel,
        grid=(M_TOT // TM_TOK, D_FF // TF_MLP),
        in_specs=[
            pl.BlockSpec((TM_TOK, D_MODEL), lambda i, f: (i, 0)),
            pl.BlockSpec((1, D_MODEL), lambda i, f: (0, 0)),
            pl.BlockSpec((D_MODEL, TF_MLP), lambda i, f: (0, f)),
            pl.BlockSpec((TF_MLP, D_MODEL), lambda i, f: (f, 0)),
        ],
        out_specs=pl.BlockSpec((TM_TOK, D_MODEL), lambda i, f: (i, 0)),
        out_shape=jax.ShapeDtypeStruct((M_TOT, D_MODEL), F32),
        scratch_shapes=[pltpu.VMEM((TM_TOK, D_MODEL), BF16)],
        compiler_params=pltpu.CompilerParams(
            dimension_semantics=("arbitrary", "arbitrary"), vmem_limit_bytes=VMEM_LIMIT),
        name="mlp",
    )(h, norm_w, w1, w2)


def _final_kernel(h_ref, nw_ref, o_ref):
    o_ref[...] = _rms_scale(h_ref[LP - SEQ:, :]) * nw_ref[...]


def _final(h3, norm_w):
    return pl.pallas_call(
        _final_kernel,
        grid=(BATCH,),
        in_specs=[
            pl.BlockSpec((None, LP, D_MODEL), lambda b: (b, 0, 0)),
            pl.BlockSpec((1, D_MODEL), lambda b: (0, 0)),
        ],
        out_specs=pl.BlockSpec((None, SEQ, D_MODEL), lambda b: (b, 0, 0)),
        out_shape=jax.ShapeDtypeStruct((BATCH, SEQ, D_MODEL), F32),
        compiler_params=pltpu.CompilerParams(
            dimension_semantics=("arbitrary",), vmem_limit_bytes=VMEM_LIMIT),
        name="final_norm",
    )(h3, norm_w)


def _pad_lanes(v):
    return jnp.pad(v, (0, LANE - v.shape[0])).reshape(1, LANE)


def _block_diag(blocks):
    n, r, c = blocks.shape
    eye = jnp.eye(n, dtype=blocks.dtype)
    return (blocks[:, :, None, :] * eye[:, None, :, None]).reshape(n * r, n * c)


def kernel(x, meta_tokens, w_in, conv_w, conv_b, dt_bias, ssd_a_log, ssd_d, ssd_norm_w, s5_a_re, s5_a_im, s5_log_step, s5_b_re, s5_b_im, s5_c_re, s5_c_im, s5_d, w_glu, w_out, norm_mix_w, norm_mlp_w, w_ff_in, w_ff_out, final_norm_w):
    bsz = x.shape[0]
    meta = jnp.broadcast_to(meta_tokens[None].astype(x.dtype), (bsz, N_META, D_MODEL))
    pad = jnp.zeros((bsz, PAD_ROWS, D_MODEL), x.dtype)
    h = jnp.concatenate([pad, meta, x], axis=1).reshape(M_TOT, D_MODEL)

    o_xbc = SSD_INNER
    o_dt = o_xbc + XBC_WIDTH
    o_u = o_dt + SSD_HEADS
    o_g = o_u + S5_WIDTH
    n_half = 2
    gph = S5_GROUPS // n_half

    for i in range(DEPTH):
        wi = w_in[i]
        w_x = wi[:, o_xbc:o_dt].astype(BF16)
        w_o = jnp.concatenate([wi[:, o_g:], wi[:, :o_xbc], wi[:, o_u:o_g]], axis=1).astype(BF16)
        w_dt = jnp.pad(wi[:, o_dt:o_u], ((0, 0), (0, LANE - SSD_HEADS))).astype(BF16)
        act3, oth3, dt3 = _inproj(
            h.reshape(BATCH, LP, D_MODEL), norm_mix_w[i].reshape(1, D_MODEL), w_x, w_o, w_dt,
            conv_w[i], conv_b[i].reshape(1, XBC_WIDTH))

        y_a = _ssd(
            act3, oth3, dt3, _pad_lanes(dt_bias[i]), _pad_lanes(ssd_a_log[i]),
            jnp.repeat(ssd_d[i], SSD_HEAD_DIM).reshape(1, SSD_INNER),
            ssd_norm_w[i].reshape(1, SSD_INNER))

        ab_re, ab_im, bb_re_t, bb_im_t = _s5_disc(
            s5_a_re[i].reshape(S5_GROUPS, 1, S5_STATE), s5_a_im[i].reshape(S5_GROUPS, 1, S5_STATE),
            s5_log_step[i].reshape(S5_GROUPS, 1, 1),
            jnp.swapaxes(s5_b_re[i], 1, 2), jnp.swapaxes(s5_b_im[i], 1, 2))
        bd_b = lambda bt: jnp.stack(
            [_block_diag(bt[k * gph:(k + 1) * gph]) for k in range(n_half)]).astype(BF16)
        gpc = S5_CTILE // S5_STATE
        bd_c = lambda cm: jnp.stack(
            [_block_diag(jnp.swapaxes(cm, 1, 2)[j * gpc:(j + 1) * gpc])
             for j in range(S5_GROUPS // gpc)]).astype(BF16)
        y_s5 = _s5(
            oth3, bd_b(bb_re_t), bd_b(bb_im_t),
            ab_re.reshape(1, S5_LANES), ab_im.reshape(1, S5_LANES),
            bd_c(s5_c_re[i]), bd_c(s5_c_im[i]))

        h = _merge(oth3.reshape(M_TOT, OTH_W), y_a.reshape(M_TOT, SSD_INNER), y_s5.reshape(M_TOT, S5_WIDTH), h,
                   s5_d[i].reshape(1, S5_WIDTH), w_glu[i].astype(BF16), w_out[i].astype(BF16))
        h = _mlp(h, norm_mlp_w[i].reshape(1, D_MODEL), w_ff_in[i].astype(BF16), w_ff_out[i].astype(BF16))

    return _final(h.reshape(BATCH, LP, D_MODEL), final_norm_w.reshape(1, D_MODEL))
```

```python
import functools
import math
from typing import NamedTuple

import jax
import jax.numpy as jnp
from jax import lax
from jax.experimental import pallas as pl
from jax.experimental.pallas import tpu as pltpu

F32 = jnp.float32
BF16 = jnp.bfloat16

D_MODEL = 1024
BATCH = 32
SEQ = 2048
DEPTH = 2
N_META = 16
SSD_HEADS = 16
SSD_HEAD_DIM = 64
SSD_INNER = SSD_HEADS * SSD_HEAD_DIM
SSD_GROUPS = 4
SSD_HEADS_PER_GROUP = SSD_HEADS // SSD_GROUPS
SSD_GROUP_W = SSD_INNER // SSD_GROUPS
SSD_STATE = 128
SSD_CONV = 4
S5_WIDTH = D_MODEL // 2
S5_GROUP = 16
S5_GROUPS = S5_WIDTH // S5_GROUP
S5_STATE = 64
S5_LANES = S5_GROUPS * S5_STATE
D_FF = 4 * D_MODEL
EPS = 1e-6
XBC_WIDTH = SSD_INNER + 2 * SSD_GROUPS * SSD_STATE

LANE = 128
SUBLANE = 8

SSD_T = 128
S5_SUB = 2
S5_TC = S5_SUB * SUBLANE
S5_CTILE = 256
S5_USLABS = S5_WIDTH // LANE

OTH_W = 2 * D_MODEL + SSD_INNER + S5_WIDTH
NJ_PROJ = 4
TN_ACT = XBC_WIDTH // NJ_PROJ
TN_OTH = OTH_W // NJ_PROJ
TM_PROJ = 1024
PROJ_CHUNK = 256
CONV_ROWS = 64
TM_TOK = 1024
TM_MERGE = 512
TF_MLP = 1024
VMEM_LIMIT = 48 * 1024 * 1024


class Dims(NamedTuple):
    nb: int
    lp: int
    pad: int


MAIN = Dims(BATCH, SEQ, 0)
PREFIX = Dims(SUBLANE, SSD_T, SSD_T - N_META)


def _sigmoid(x):
    return 0.5 * (jnp.tanh(0.5 * x) + 1.0)


def _rms_scale(x):
    return x * lax.rsqrt(jnp.mean(x * x, axis=-1, keepdims=True) + EPS)


def _bdot(a, b):
    return jnp.dot(a, b, preferred_element_type=F32)


def _params(n_axes):
    return pltpu.CompilerParams(
        dimension_semantics=("arbitrary",) * n_axes, vmem_limit_bytes=VMEM_LIMIT)


def _inproj_kernel(tm, h_ref, nw_ref, wx_ref, wo_ref, wdt_ref, cw_ref, cb_ref, cin_ref,
                   act_ref, oth_ref, dt_ref, tail_ref, xn_ref, ext_ref, carry_ref):
    i = pl.program_id(1)
    j = pl.program_id(2)

    @pl.when(j == 0)
    def _():
        xn = (_rms_scale(h_ref[...]) * nw_ref[...]).astype(BF16)
        xn_ref[...] = xn
        dt_ref[...] = _bdot(xn, wdt_ref[...])

    def project(lo, hi):
        xn = xn_ref[lo:hi, :]
        ext_ref[SUBLANE + lo:SUBLANE + hi, :] = _bdot(xn, wx_ref[...])
        oth_ref[lo:hi, :] = _bdot(xn, wo_ref[...]).astype(BF16)

    def conv_silu(lo, hi):
        for r0 in range(lo, hi, CONV_ROWS):
            for lb in range(TN_ACT // LANE):
                ls = slice(lb * LANE, (lb + 1) * LANE)
                xe = ext_ref[r0:r0 + SUBLANE + CONV_ROWS, ls]
                acc = cb_ref[:, ls] + cw_ref[SSD_CONV - 1:SSD_CONV, ls] * xe[SUBLANE:, :]
                for k in range(SSD_CONV - 1):
                    shifted = pltpu.roll(xe, SSD_CONV - 1 - k, 0)[SUBLANE:, :]
                    acc = acc + cw_ref[k:k + 1, ls] * shifted
                half = 0.5 * acc
                act_ref[r0:r0 + CONV_ROWS, ls] = (half * (jnp.tanh(half) + 1.0)).astype(BF16)

    ext_ref[0:SUBLANE, :] = jnp.where(i == 0, cin_ref[...], carry_ref[j])
    bounds = tuple(range(0, tm, PROJ_CHUNK)) + (tm,)
    project(bounds[0], bounds[1])
    for q in range(1, len(bounds) - 1):
        conv_silu(bounds[q - 1], bounds[q])
        project(bounds[q], bounds[q + 1])
    conv_silu(bounds[-2], bounds[-1])
    tail = ext_ref[tm:tm + SUBLANE, :]
    carry_ref[j] = tail
    tail_ref[...] = tail


def _inproj(dims, h3, norm_w, w_x, w_o, w_dt, conv_w, conv_b, conv_tail):
    tm = min(TM_PROJ, dims.lp)
    c2 = lambda b, i, j: (0, 0)
    sds = jax.ShapeDtypeStruct
    return pl.pallas_call(
        functools.partial(_inproj_kernel, tm),
        grid=(dims.nb, dims.lp // tm, NJ_PROJ),
        in_specs=[
            pl.BlockSpec((None, tm, D_MODEL), lambda b, i, j: (b, i, 0)),
            pl.BlockSpec((1, D_MODEL), c2),
            pl.BlockSpec((D_MODEL, TN_ACT), lambda b, i, j: (0, j)),
            pl.BlockSpec((D_MODEL, TN_OTH), lambda b, i, j: (0, j)),
            pl.BlockSpec((D_MODEL, LANE), c2),
            pl.BlockSpec((SSD_CONV, TN_ACT), lambda b, i, j: (0, j)),
            pl.BlockSpec((1, TN_ACT), lambda b, i, j: (0, j)),
            pl.BlockSpec((SUBLANE, TN_ACT), lambda b, i, j: (0, j)),
        ],
        out_specs=[
            pl.BlockSpec((None, tm, TN_ACT), lambda b, i, j: (b, i, j)),
            pl.BlockSpec((None, tm, TN_OTH), lambda b, i, j: (b, i, j)),
            pl.BlockSpec((None, tm, LANE), lambda b, i, j: (b, i, 0)),
            pl.BlockSpec((None, None, SUBLANE, TN_ACT), lambda b, i, j: (b, i, 0, j)),
        ],
        out_shape=[
            sds((dims.nb, dims.lp, XBC_WIDTH), BF16),
            sds((dims.nb, dims.lp, OTH_W), BF16),
            sds((dims.nb, dims.lp, LANE), F32),
            sds((dims.nb, dims.lp // tm, SUBLANE, XBC_WIDTH), F32),
        ],
        scratch_shapes=[
            pltpu.VMEM((tm, D_MODEL), BF16),
            pltpu.VMEM((tm + SUBLANE, TN_ACT), F32),
            pltpu.VMEM((NJ_PROJ, SUBLANE, TN_ACT), F32),
        ],
        compiler_params=_params(3),
        name="inproj",
    )(h3, norm_w, w_x, w_o, w_dt, conv_w, conv_b, conv_tail)


def _split_bf16(x, n):
    parts = []
    for _ in range(n):
        p = x.astype(BF16)
        parts.append(p)
        x = x - p.astype(F32)
    return jnp.concatenate(parts, axis=1)


def _ssd_kernel(pad, act_ref, z_ref, dt_ref, dtb_ref, alog_ref, exp_ref, hmask_ref, dsk_ref, nw_ref,
                stin_ref, y_ref, stout_ref, st_ref):
    c = pl.program_id(1)
    t = SSD_T

    @pl.when(c == 0)
    def _():
        st_ref[...] = stin_ref[...]

    xdt = dt_ref[...] + dtb_ref[...]
    dt = jnp.maximum(xdt, 0.0) + jnp.log1p(jnp.exp(-jnp.abs(xdt)))
    if pad:
        row = c * t + lax.broadcasted_iota(jnp.int32, (t, LANE), 0)
        dt = jnp.where(row >= pad, dt, 0.0)
    a = -jnp.exp(alog_ref[...])
    r_i = lax.broadcasted_iota(jnp.int32, (t, t), 0)
    c_i = lax.broadcasted_iota(jnp.int32, (t, t), 1)
    causal = r_i >= c_i
    cs = jnp.dot(causal.astype(F32), dt * a, preferred_element_type=F32,
                 precision=lax.Precision.HIGHEST)
    cs_t = cs.T

    dt_ch = _bdot(_split_bf16(dt, 2), exp_ref[0:2 * LANE, :])
    cs_ch = _bdot(_split_bf16(cs, 3), exp_ref[...])
    cs_end = cs_ch[t - 1:t, :]
    xs = act_ref[:, :SSD_INNER].astype(F32)
    xdt = xs * dt_ch
    xdt_b = xdt.astype(BF16)
    xdt_end_b = (xdt * jnp.exp(cs_end - cs_ch)).astype(BF16)

    for g in range(SSD_GROUPS):
        gs = slice(g * SSD_GROUP_W, (g + 1) * SSD_GROUP_W)
        b_lo = SSD_INNER + g * SSD_STATE
        c_lo = SSD_INNER + SSD_GROUPS * SSD_STATE + g * SSD_STATE
        b_gt = act_ref[:, b_lo:b_lo + SSD_STATE].astype(F32).T.astype(BF16)
        c_g = act_ref[:, c_lo:c_lo + SSD_STATE]
        scores = _bdot(c_g, b_gt)
        m_parts = []
        for r in range(SSD_HEADS_PER_GROUP):
            h = g * SSD_HEADS_PER_GROUP + r
            seg = cs[:, h:h + 1] - cs_t[h:h + 1, :]
            decay = jnp.exp(jnp.where(causal, seg, -jnp.inf))
            m_parts.append((scores * decay).astype(BF16))
        x_bd = jnp.concatenate([xdt_b[:, gs]] * SSD_HEADS_PER_GROUP, axis=0) * hmask_ref[...]
        y = _bdot(jnp.concatenate(m_parts, axis=1), x_bd)
        st = st_ref[g]
        y = y + _bdot(c_g, st.astype(BF16)) * jnp.exp(cs_ch[:, gs])
        st_ref[g] = st * jnp.exp(cs_end[:, gs]) + _bdot(b_gt, xdt_end_b[:, gs])

        z = z_ref[:, gs].astype(F32)
        y = (y + dsk_ref[:, gs] * xs[:, gs]) * (z * _sigmoid(z))
        y_ref[:, gs] = (_rms_scale(y) * nw_ref[:, gs]).astype(BF16)

    @pl.when(c == pl.num_programs(1) - 1)
    def _():
        stout_ref[...] = st_ref[...]


def _ssd(dims, act3, oth3, dt3, dt_bias, a_log, d_full, norm_w, state_in):
    z_blk = 2 * D_MODEL // SSD_INNER
    c2 = lambda b, c: (0, 0)
    st_shape = (SSD_GROUPS, SSD_STATE, SSD_GROUP_W)
    head_of_ch = jnp.arange(SSD_INNER) // SSD_HEAD_DIM
    expand = (jnp.arange(LANE)[:, None] == head_of_ch[None, :]).astype(BF16)
    expand3 = jnp.tile(expand, (3, 1))
    head_of_row = jnp.arange(SSD_HEADS_PER_GROUP * SSD_T) // SSD_T
    hmask = (head_of_row[:, None] == head_of_ch[None, :SSD_GROUP_W]).astype(BF16)
    return pl.pallas_call(
        functools.partial(_ssd_kernel, dims.pad),
        grid=(dims.nb, dims.lp // SSD_T),
        in_specs=[
            pl.BlockSpec((None, SSD_T, XBC_WIDTH), lambda b, c: (b, c, 0)),
            pl.BlockSpec((None, SSD_T, SSD_INNER), lambda b, c: (b, c, z_blk)),
            pl.BlockSpec((None, SSD_T, LANE), lambda b, c: (b, c, 0)),
            pl.BlockSpec((1, LANE), c2),
            pl.BlockSpec((1, LANE), c2),
            pl.BlockSpec((3 * LANE, SSD_INNER), c2),
            pl.BlockSpec((SSD_HEADS_PER_GROUP * SSD_T, SSD_GROUP_W), c2),
            pl.BlockSpec((1, SSD_INNER), c2),
            pl.BlockSpec((1, SSD_INNER), c2),
            pl.BlockSpec(st_shape, lambda b, c: (0, 0, 0)),
        ],
        out_specs=[
            pl.BlockSpec((None, SSD_T, SSD_INNER), lambda b, c: (b, c, 0)),
            pl.BlockSpec((None,) + st_shape, lambda b, c: (b, 0, 0, 0)),
        ],
        out_shape=[
            jax.ShapeDtypeStruct((dims.nb, dims.lp, SSD_INNER), BF16),
            jax.ShapeDtypeStruct((dims.nb,) + st_shape, F32),
        ],
        scratch_shapes=[pltpu.VMEM(st_shape, F32)],
        compiler_params=_params(2),
        name="ssd",
    )(act3, oth3, dt3, dt_bias, a_log, expand3, hmask, d_full, norm_w, state_in)


def _s5_disc_kernel(are_ref, aim_ref, ls_ref, bre_ref, bim_ref, abre_ref, abim_ref, bbre_ref, bbim_ref):
    step = jnp.exp(ls_ref[...])
    lam_re = are_ref[...]
    lam_im = aim_ref[...]
    mag = jnp.exp(lam_re * step)
    ab_re = mag * jnp.cos(lam_im * step)
    ab_im = mag * jnp.sin(lam_im * step)
    den = lam_re * lam_re + lam_im * lam_im
    nr = ab_re - 1.0
    f_re = (nr * lam_re + ab_im * lam_im) / den
    f_im = (ab_im * lam_re - nr * lam_im) / den
    br = bre_ref[...]
    bi = bim_ref[...]
    abre_ref[...] = ab_re
    abim_ref[...] = ab_im
    bbre_ref[...] = f_re * br - f_im * bi
    bbim_ref[...] = f_re * bi + f_im * br


def _s5_disc(a_re, a_im, log_step, b_re_t, b_im_t):
    sds = jax.ShapeDtypeStruct
    return pl.pallas_call(
        _s5_disc_kernel,
        out_shape=[sds(a_re.shape, F32), sds(a_re.shape, F32), sds(b_re_t.shape, F32), sds(b_re_t.shape, F32)],
        name="s5_disc",
    )(a_re, a_im, log_step, b_re_t, b_im_t)


def _s5_kernel(nb, u_ref, bre_ref, bim_ref, are_ref, aim_ref, cre_ref, cim_ref, hin_ref,
               y_ref, hout_ref, uslab_ref, yslab_ref, bu_ref, h_ref):
    @pl.when(pl.program_id(0) == 0)
    def _():
        h_ref[...] = jnp.broadcast_to(hin_ref[...], h_ref.shape)

    blk_rows = nb * SUBLANE

    u32 = u_ref[...].astype(F32)
    for q in range(S5_SUB):
        uq = u32[:, q * SUBLANE:(q + 1) * SUBLANE, :].reshape(blk_rows, S5_WIDTH)
        for k in range(S5_USLABS):
            uslab_ref[q * S5_USLABS + k] = uq[:, k * LANE:(k + 1) * LANE]
    steps = []
    for t in range(S5_TC):
        q, tl = divmod(t, SUBLANE)
        rows = pl.ds(tl, nb, stride=SUBLANE)
        steps.append(jnp.concatenate(
            [uslab_ref[q * S5_USLABS + k, rows, :] for k in range(S5_USLABS)], axis=1))
    u = jnp.concatenate(steps, axis=0).astype(BF16)

    n_half = bre_ref.shape[0]
    kw = S5_WIDTH // n_half
    nw = S5_LANES // n_half
    for k in range(n_half):
        uk = u[:, k * kw:(k + 1) * kw]
        bu_ref[:, k * nw:(k + 1) * nw] = _bdot(uk, bre_ref[k])
        bu_ref[:, S5_LANES + k * nw:S5_LANES + (k + 1) * nw] = _bdot(uk, bim_ref[k])

    for s in range(S5_LANES // LANE):
        ls = slice(s * LANE, (s + 1) * LANE)
        li = slice(S5_LANES + s * LANE, S5_LANES + (s + 1) * LANE)
        a_r = are_ref[:, ls]
        a_i = aim_ref[:, ls]
        h_r = h_ref[:, ls]
        h_i = h_ref[:, li]
        for t in range(S5_TC):
            rows = slice(t * nb, (t + 1) * nb)
            n_r = a_r * h_r - a_i * h_i + bu_ref[rows, ls]
            n_i = a_r * h_i + a_i * h_r + bu_ref[rows, li]
            h_r, h_i = n_r, n_i
            bu_ref[rows, ls] = h_r
            bu_ref[rows, li] = h_i
        h_ref[:, ls] = h_r
        h_ref[:, li] = h_i

    pieces = []
    for j in range(cre_ref.shape[0]):
        ks = slice(j * S5_CTILE, (j + 1) * S5_CTILE)
        ki = slice(S5_LANES + j * S5_CTILE, S5_LANES + (j + 1) * S5_CTILE)
        pieces.append(_bdot(bu_ref[:, ks].astype(BF16), cre_ref[j])
                      - _bdot(bu_ref[:, ki].astype(BF16), cim_ref[j]))
    y = jnp.concatenate(pieces, axis=1)

    for t in range(S5_TC):
        q, tl = divmod(t, SUBLANE)
        rows = pl.ds(tl, nb, stride=SUBLANE)
        for k in range(S5_USLABS):
            yslab_ref[q * S5_USLABS + k, rows, :] = y[t * nb:(t + 1) * nb, k * LANE:(k + 1) * LANE]
    halves = []
    for q in range(S5_SUB):
        yq = jnp.concatenate([yslab_ref[q * S5_USLABS + k] for k in range(S5_USLABS)], axis=1)
        halves.append(yq.reshape(nb, SUBLANE, S5_WIDTH))
    y_ref[...] = jnp.concatenate(halves, axis=1).astype(BF16)

    @pl.when(pl.program_id(0) == pl.num_programs(0) - 1)
    def _():
        hout_ref[...] = h_ref[...]


def _s5(dims, oth3, bre_bd, bim_bd, ab_re, ab_im, cre_bd, cim_bd, state_in):
    nb = dims.nb
    u_blk = (OTH_W - S5_WIDTH) // S5_WIDTH
    c2 = lambda i: (0, 0)
    c3 = lambda i: (0, 0, 0)
    return pl.pallas_call(
        functools.partial(_s5_kernel, nb),
        grid=(dims.lp // S5_TC,),
        in_specs=[
            pl.BlockSpec((nb, S5_TC, S5_WIDTH), lambda i: (0, i, u_blk)),
            pl.BlockSpec(bre_bd.shape, c3),
            pl.BlockSpec(bim_bd.shape, c3),
            pl.BlockSpec((1, S5_LANES), c2),
            pl.BlockSpec((1, S5_LANES), c2),
            pl.BlockSpec(cre_bd.shape, c3),
            pl.BlockSpec(cim_bd.shape, c3),
            pl.BlockSpec((1, 2 * S5_LANES), c2),
        ],
        out_specs=[
            pl.BlockSpec((nb, S5_TC, S5_WIDTH), lambda i: (0, i, 0)),
            pl.BlockSpec((nb, 2 * S5_LANES), c2),
        ],
        out_shape=[
            jax.ShapeDtypeStruct((nb, dims.lp, S5_WIDTH), BF16),
            jax.ShapeDtypeStruct((nb, 2 * S5_LANES), F32),
        ],
        scratch_shapes=[
            pltpu.VMEM((S5_SUB * S5_USLABS, nb * SUBLANE, LANE), F32),
            pltpu.VMEM((S5_SUB * S5_USLABS, nb * SUBLANE, LANE), F32),
            pltpu.VMEM((nb * S5_TC, 2 * S5_LANES), F32),
            pltpu.VMEM((nb, 2 * S5_LANES), F32),
        ],
        compiler_params=_params(1),
        name="s5",
    )(oth3, bre_bd, bim_bd, ab_re, ab_im, cre_bd, cim_bd, state_in)


def _merge_kernel(g_ref, ya_ref, ys_ref, u_ref, h_ref, sd_ref, wglu_ref, wout_ref, o_ref):
    ys = ys_ref[...].astype(F32) + sd_ref[...] * u_ref[...].astype(F32)
    cdf = 0.5 * (1.0 + jnp.tanh(math.sqrt(2.0 / math.pi) * (ys + 0.044715 * (ys * ys * ys))))
    vg = _bdot((ys * cdf).astype(BF16), wglu_ref[...])
    yb = vg[:, :D_MODEL] * _sigmoid(vg[:, D_MODEL:])
    g = g_ref[...].astype(F32)
    m = _sigmoid(g[:, :D_MODEL]) * ya_ref[...].astype(F32) + _sigmoid(g[:, D_MODEL:]) * yb
    o_ref[...] = h_ref[...] + _bdot(m.astype(BF16), wout_ref[...])


def _merge(oth, y_a, y_s5, h, s5_d, w_glu, w_out):
    rows = h.shape[0]
    tm = min(TM_MERGE, rows)
    u_blk = (OTH_W - S5_WIDTH) // S5_WIDTH
    const = lambda i: (0, 0)
    return pl.pallas_call(
        _merge_kernel,
        grid=(rows // tm,),
        in_specs=[
            pl.BlockSpec((tm, 2 * D_MODEL), lambda i: (i, 0)),
            pl.BlockSpec((tm, SSD_INNER), lambda i: (i, 0)),
            pl.BlockSpec((tm, S5_WIDTH), lambda i: (i, 0)),
            pl.BlockSpec((tm, S5_WIDTH), lambda i: (i, u_blk)),
            pl.BlockSpec((tm, D_MODEL), lambda i: (i, 0)),
            pl.BlockSpec((1, S5_WIDTH), const),
            pl.BlockSpec((S5_WIDTH, 2 * D_MODEL), const),
            pl.BlockSpec((D_MODEL, D_MODEL), const),
        ],
        out_specs=pl.BlockSpec((tm, D_MODEL), lambda i: (i, 0)),
        out_shape=jax.ShapeDtypeStruct((rows, D_MODEL), F32),
        compiler_params=_params(1),
        name="merge",
    )(oth, y_a, y_s5, oth, h, s5_d, w_glu, w_out)


def _mlp_kernel(final, h_ref, nw_ref, w1_ref, w2_ref, fw_ref, o_ref, xn_ref):
    f = pl.program_id(1)

    @pl.when(f == 0)
    def _():
        x = h_ref[...]
        xn_ref[...] = (_rms_scale(x) * nw_ref[...]).astype(BF16)
        o_ref[...] = x

    hid = jnp.square(jnp.maximum(_bdot(xn_ref[...], w1_ref[...]), 0.0))
    o_ref[...] += _bdot(hid.astype(BF16), w2_ref[...])

    if final:
        @pl.when(f == pl.num_programs(1) - 1)
        def _():
            o_ref[...] = _rms_scale(o_ref[...]) * fw_ref[...]


def _mlp(h, norm_w, w1, w2, final_w, final):
    rows = h.shape[0]
    tm = min(TM_TOK, rows)
    return pl.pallas_call(
        functools.partial(_mlp_kernel, final),
        grid=(rows // tm, D_FF // TF_MLP),
        in_specs=[
            pl.BlockSpec((tm, D_MODEL), lambda i, f: (i, 0)),
            pl.BlockSpec((1, D_MODEL), lambda i, f: (0, 0)),
            pl.BlockSpec((D_MODEL, TF_MLP), lambda i, f: (0, f)),
            pl.BlockSpec((TF_MLP, D_MODEL), lambda i, f: (f, 0)),
            pl.BlockSpec((1, D_MODEL), lambda i, f: (0, 0)),
        ],
        out_specs=pl.BlockSpec((tm, D_MODEL), lambda i, f: (i, 0)),
        out_shape=jax.ShapeDtypeStruct((rows, D_MODEL), F32),
        scratch_shapes=[pltpu.VMEM((tm, D_MODEL), BF16)],
        compiler_params=_params(2),
        name="mlp",
    )(h, norm_w, w1, w2, final_w)


def _pad_lanes(v):
    return jnp.pad(v, (0, LANE - v.shape[0])).reshape(1, LANE)


def _block_diag(blocks):
    n, r, c = blocks.shape
    eye = jnp.eye(n, dtype=blocks.dtype)
    return (blocks[:, :, None, :] * eye[:, None, :, None]).reshape(n * r, n * c)


def _mixers(dims, h, p, conv_tail, ssd_state, s5_state):
    act3, oth3, dt3, tail = _inproj(
        dims, h.reshape(dims.nb, dims.lp, D_MODEL), p["norm_mix_w"], p["w_x"], p["w_o"], p["w_dt"],
        p["conv_w"], p["conv_b"], conv_tail)
    y_a, st = _ssd(dims, act3, oth3, dt3, p["dt_bias"], p["a_log"], p["d_full"], p["ssd_norm_w"],
                   ssd_state)
    y_s5, hs = _s5(dims, oth3, p["bre_bd"], p["bim_bd"], p["ab_re"], p["ab_im"], p["cre_bd"],
                   p["cim_bd"], s5_state)
    rows = dims.nb * dims.lp
    h = _merge(oth3.reshape(rows, OTH_W), y_a.reshape(rows, SSD_INNER), y_s5.reshape(rows, S5_WIDTH),
               h, p["s5_d"], p["w_glu"], p["w_out"])
    return h, tail, st, hs


def kernel(x, meta_tokens, w_in, conv_w, conv_b, dt_bias, ssd_a_log, ssd_d, ssd_norm_w, s5_a_re, s5_a_im, s5_log_step, s5_b_re, s5_b_im, s5_c_re, s5_c_im, s5_d, w_glu, w_out, norm_mix_w, norm_mlp_w, w_ff_in, w_ff_out, final_norm_w):
    o_xbc = SSD_INNER
    o_dt = o_xbc + XBC_WIDTH
    o_u = o_dt + SSD_HEADS
    o_g = o_u + S5_WIDTH
    n_half = 2
    gph = S5_GROUPS // n_half
    gpc = S5_CTILE // S5_STATE
    final_w = final_norm_w.reshape(1, D_MODEL)

    h = x.reshape(BATCH * SEQ, D_MODEL)
    meta_seq = jnp.concatenate(
        [jnp.zeros((PREFIX.pad, D_MODEL), x.dtype), meta_tokens.astype(x.dtype)], axis=0)
    h_pre = jnp.broadcast_to(meta_seq[None], (PREFIX.nb, PREFIX.lp, D_MODEL)).reshape(-1, D_MODEL)
    zero_tail = jnp.zeros((SUBLANE, XBC_WIDTH), F32)
    zero_ssd = jnp.zeros((SSD_GROUPS, SSD_STATE, SSD_GROUP_W), F32)
    zero_s5 = jnp.zeros((1, 2 * S5_LANES), F32)

    for i in range(DEPTH):
        wi = w_in[i]
        ab_re, ab_im, bb_re_t, bb_im_t = _s5_disc(
            s5_a_re[i].reshape(S5_GROUPS, 1, S5_STATE), s5_a_im[i].reshape(S5_GROUPS, 1, S5_STATE),
            s5_log_step[i].reshape(S5_GROUPS, 1, 1),
            jnp.swapaxes(s5_b_re[i], 1, 2), jnp.swapaxes(s5_b_im[i], 1, 2))
        bd_b = lambda bt: jnp.stack(
            [_block_diag(bt[k * gph:(k + 1) * gph]) for k in range(n_half)]).astype(BF16)
        bd_c = lambda cm: jnp.stack(
            [_block_diag(jnp.swapaxes(cm, 1, 2)[j * gpc:(j + 1) * gpc])
             for j in range(S5_GROUPS // gpc)]).astype(BF16)
        p = {
            "norm_mix_w": norm_mix_w[i].reshape(1, D_MODEL),
            "w_x": wi[:, o_xbc:o_dt].astype(BF16),
            "w_o": jnp.concatenate([wi[:, o_g:], wi[:, :o_xbc], wi[:, o_u:o_g]], axis=1).astype(BF16),
            "w_dt": jnp.pad(wi[:, o_dt:o_u], ((0, 0), (0, LANE - SSD_HEADS))).astype(BF16),
            "conv_w": conv_w[i],
            "conv_b": conv_b[i].reshape(1, XBC_WIDTH),
            "dt_bias": _pad_lanes(dt_bias[i]),
            "a_log": _pad_lanes(ssd_a_log[i]),
            "d_full": jnp.repeat(ssd_d[i], SSD_HEAD_DIM).reshape(1, SSD_INNER),
            "ssd_norm_w": ssd_norm_w[i].reshape(1, SSD_INNER),
            "bre_bd": bd_b(bb_re_t),
            "bim_bd": bd_b(bb_im_t),
            "ab_re": ab_re.reshape(1, S5_LANES),
            "ab_im": ab_im.reshape(1, S5_LANES),
            "cre_bd": bd_c(s5_c_re[i]),
            "cim_bd": bd_c(s5_c_im[i]),
            "s5_d": s5_d[i].reshape(1, S5_WIDTH),
            "w_glu": w_glu[i].astype(BF16),
            "w_out": w_out[i].astype(BF16),
        }
        mlp_w = (norm_mlp_w[i].reshape(1, D_MODEL), w_ff_in[i].astype(BF16), w_ff_out[i].astype(BF16))
        last = i == DEPTH - 1

        h_pre, tail, st, hs = _mixers(PREFIX, h_pre, p, zero_tail, zero_ssd, zero_s5)
        if not last:
            h_pre = _mlp(h_pre, *mlp_w, final_w, False)
        h, _, _, _ = _mixers(MAIN, h, p, tail[0, -1], st[0], hs[0:1])
        h = _mlp(h, *mlp_w, final_w, last)

    return h.reshape(BATCH, SEQ, D_MODEL)
```

```python
import functools
import math
from typing import NamedTuple

import jax
import jax.numpy as jnp
from jax import lax
from jax.experimental import pallas as pl
from jax.experimental.pallas import tpu as pltpu

F32 = jnp.float32
BF16 = jnp.bfloat16

D_MODEL = 1024
BATCH = 32
SEQ = 2048
DEPTH = 2
N_META = 16
SSD_HEADS = 16
SSD_HEAD_DIM = 64
SSD_INNER = SSD_HEADS * SSD_HEAD_DIM
SSD_GROUPS = 4
SSD_HEADS_PER_GROUP = SSD_HEADS // SSD_GROUPS
SSD_GROUP_W = SSD_INNER // SSD_GROUPS
SSD_STATE = 128
SSD_CONV = 4
S5_WIDTH = D_MODEL // 2
S5_GROUP = 16
S5_GROUPS = S5_WIDTH // S5_GROUP
S5_STATE = 64
S5_LANES = S5_GROUPS * S5_STATE
D_FF = 4 * D_MODEL
EPS = 1e-6
XBC_WIDTH = SSD_INNER + 2 * SSD_GROUPS * SSD_STATE

LANE = 128
SUBLANE = 8

SSD_T = 128
S5_SUB = 2
S5_TC = S5_SUB * SUBLANE
S5_CTILE = 256
S5_USLABS = S5_WIDTH // LANE

OTH_W = 2 * D_MODEL + SSD_INNER + S5_WIDTH
NJ_PROJ = 4
TN_ACT = XBC_WIDTH // NJ_PROJ
TN_OTH = OTH_W // NJ_PROJ
TM_PROJ = 2048
PROJ_CHUNK = 256
CONV_ROWS = 64
TM_TOK = 1024
TM_MERGE = 512
TF_MLP = 1024
VMEM_LIMIT = 48 * 1024 * 1024


class Dims(NamedTuple):
    nb: int
    lp: int
    pad: int


MAIN = Dims(BATCH, SEQ, 0)
PREFIX = Dims(1, SSD_T, SSD_T - N_META)


def _sigmoid(x):
    return 0.5 * (jnp.tanh(0.5 * x) + 1.0)


def _rms_scale(x):
    return x * lax.rsqrt(jnp.mean(x * x, axis=-1, keepdims=True) + EPS)


def _bdot(a, b):
    return jnp.dot(a, b, preferred_element_type=F32)


def _params(n_axes):
    return pltpu.CompilerParams(
        dimension_semantics=("arbitrary",) * n_axes, vmem_limit_bytes=VMEM_LIMIT)


def _inproj_kernel(tm, h_ref, nw_ref, wx_ref, wo_ref, wdt_ref, cw_ref, cb_ref, cin_ref,
                   act_ref, oth_ref, dt_ref, tail_ref, xn_ref, ext_ref, carry_ref):
    i = pl.program_id(1)
    j = pl.program_id(2)

    @pl.when(j == 0)
    def _():
        xn = (_rms_scale(h_ref[...]) * nw_ref[...]).astype(BF16)
        xn_ref[...] = xn
        dt_ref[...] = _bdot(xn, wdt_ref[...])

    def project(lo, hi):
        xn = xn_ref[lo:hi, :]
        ext_ref[SUBLANE + lo:SUBLANE + hi, :] = _bdot(xn, wx_ref[...])
        oth_ref[lo:hi, :] = _bdot(xn, wo_ref[...]).astype(BF16)

    def conv_silu(lo, hi):
        for r0 in range(lo, hi, CONV_ROWS):
            for lb in range(TN_ACT // LANE):
                ls = slice(lb * LANE, (lb + 1) * LANE)
                xe = ext_ref[r0:r0 + SUBLANE + CONV_ROWS, ls]
                acc = cb_ref[:, ls] + cw_ref[SSD_CONV - 1:SSD_CONV, ls] * xe[SUBLANE:, :]
                for k in range(SSD_CONV - 1):
                    shifted = pltpu.roll(xe, SSD_CONV - 1 - k, 0)[SUBLANE:, :]
                    acc = acc + cw_ref[k:k + 1, ls] * shifted
                half = 0.5 * acc
                act_ref[r0:r0 + CONV_ROWS, ls] = (half * (jnp.tanh(half) + 1.0)).astype(BF16)

    ext_ref[0:SUBLANE, :] = jnp.where(i == 0, cin_ref[...], carry_ref[j])
    bounds = tuple(range(0, tm, PROJ_CHUNK)) + (tm,)
    project(bounds[0], bounds[1])
    for q in range(1, len(bounds) - 1):
        conv_silu(bounds[q - 1], bounds[q])
        project(bounds[q], bounds[q + 1])
    conv_silu(bounds[-2], bounds[-1])
    tail = ext_ref[tm:tm + SUBLANE, :]
    carry_ref[j] = tail
    tail_ref[...] = tail


def _inproj(dims, h3, norm_w, w_x, w_o, w_dt, conv_w, conv_b, conv_tail):
    tm = min(TM_PROJ, dims.lp)
    c2 = lambda b, i, j: (0, 0)
    sds = jax.ShapeDtypeStruct
    return pl.pallas_call(
        functools.partial(_inproj_kernel, tm),
        grid=(dims.nb, dims.lp // tm, NJ_PROJ),
        in_specs=[
            pl.BlockSpec((None, tm, D_MODEL), lambda b, i, j: (b, i, 0)),
            pl.BlockSpec((1, D_MODEL), c2),
            pl.BlockSpec((D_MODEL, TN_ACT), lambda b, i, j: (0, j)),
            pl.BlockSpec((D_MODEL, TN_OTH), lambda b, i, j: (0, j)),
            pl.BlockSpec((D_MODEL, LANE), c2),
            pl.BlockSpec((SSD_CONV, TN_ACT), lambda b, i, j: (0, j)),
            pl.BlockSpec((1, TN_ACT), lambda b, i, j: (0, j)),
            pl.BlockSpec((SUBLANE, TN_ACT), lambda b, i, j: (0, j)),
        ],
        out_specs=[
            pl.BlockSpec((None, tm, TN_ACT), lambda b, i, j: (b, i, j)),
            pl.BlockSpec((None, tm, TN_OTH), lambda b, i, j: (b, i, j)),
            pl.BlockSpec((None, tm, LANE), lambda b, i, j: (b, i, 0)),
            pl.BlockSpec((None, None, SUBLANE, TN_ACT), lambda b, i, j: (b, i, 0, j)),
        ],
        out_shape=[
            sds((dims.nb, dims.lp, XBC_WIDTH), BF16),
            sds((dims.nb, dims.lp, OTH_W), BF16),
            sds((dims.nb, dims.lp, LANE), F32),
            sds((dims.nb, dims.lp // tm, SUBLANE, XBC_WIDTH), F32),
        ],
        scratch_shapes=[
            pltpu.VMEM((tm, D_MODEL), BF16),
            pltpu.VMEM((tm + SUBLANE, TN_ACT), F32),
            pltpu.VMEM((NJ_PROJ, SUBLANE, TN_ACT), F32),
        ],
        compiler_params=_params(3),
        name="inproj",
    )(h3, norm_w, w_x, w_o, w_dt, conv_w, conv_b, conv_tail)


def _split_bf16(x, n):
    parts = []
    for _ in range(n):
        p = x.astype(BF16)
        parts.append(p)
        x = x - p.astype(F32)
    return jnp.concatenate(parts, axis=1)


def _ssd_kernel(pad, act_ref, z_ref, dt_ref, dtb_ref, alog_ref, exp_ref, hmask_ref, dsk_ref, nw_ref,
                stin_ref, y_ref, stout_ref, st_ref):
    c = pl.program_id(1)
    t = SSD_T

    @pl.when(c == 0)
    def _():
        st_ref[...] = stin_ref[...]

    xdt = dt_ref[...] + dtb_ref[...]
    dt = jnp.maximum(xdt, 0.0) + jnp.log1p(jnp.exp(-jnp.abs(xdt)))
    if pad:
        row = c * t + lax.broadcasted_iota(jnp.int32, (t, LANE), 0)
        dt = jnp.where(row >= pad, dt, 0.0)
    a = -jnp.exp(alog_ref[...])
    r_i = lax.broadcasted_iota(jnp.int32, (t, t), 0)
    c_i = lax.broadcasted_iota(jnp.int32, (t, t), 1)
    causal = r_i >= c_i
    cs = jnp.dot(causal.astype(F32), dt * a, preferred_element_type=F32,
                 precision=lax.Precision.HIGHEST)
    cs_t = cs.T

    dt_ch = _bdot(_split_bf16(dt, 2), exp_ref[0:2 * LANE, :])
    cs_ch = _bdot(_split_bf16(cs, 3), exp_ref[...])
    cs_end = cs_ch[t - 1:t, :]
    xs = act_ref[:, :SSD_INNER].astype(F32)
    xdt = xs * dt_ch
    xdt_b = xdt.astype(BF16)
    xdt_end_b = (xdt * jnp.exp(cs_end - cs_ch)).astype(BF16)

    for g in range(SSD_GROUPS):
        gs = slice(g * SSD_GROUP_W, (g + 1) * SSD_GROUP_W)
        b_lo = SSD_INNER + g * SSD_STATE
        c_lo = SSD_INNER + SSD_GROUPS * SSD_STATE + g * SSD_STATE
        b_gt = act_ref[:, b_lo:b_lo + SSD_STATE].astype(F32).T.astype(BF16)
        c_g = act_ref[:, c_lo:c_lo + SSD_STATE]
        scores = _bdot(c_g, b_gt)
        m_parts = []
        for r in range(SSD_HEADS_PER_GROUP):
            h = g * SSD_HEADS_PER_GROUP + r
            seg = cs[:, h:h + 1] - cs_t[h:h + 1, :]
            decay = jnp.exp(jnp.where(causal, seg, -jnp.inf))
            m_parts.append((scores * decay).astype(BF16))
        x_bd = jnp.concatenate([xdt_b[:, gs]] * SSD_HEADS_PER_GROUP, axis=0) * hmask_ref[...]
        y = _bdot(jnp.concatenate(m_parts, axis=1), x_bd)
        st = st_ref[g]
        y = y + _bdot(c_g, st.astype(BF16)) * jnp.exp(cs_ch[:, gs])
        st_ref[g] = st * jnp.exp(cs_end[:, gs]) + _bdot(b_gt, xdt_end_b[:, gs])

        z = z_ref[:, gs].astype(F32)
        y = (y + dsk_ref[:, gs] * xs[:, gs]) * (z * _sigmoid(z))
        y_ref[:, gs] = (_rms_scale(y) * nw_ref[:, gs]).astype(BF16)

    @pl.when(c == pl.num_programs(1) - 1)
    def _():
        stout_ref[...] = st_ref[...]


def _ssd(dims, act3, oth3, dt3, dt_bias, a_log, d_full, norm_w, state_in):
    z_blk = 2 * D_MODEL // SSD_INNER
    c2 = lambda b, c: (0, 0)
    st_shape = (SSD_GROUPS, SSD_STATE, SSD_GROUP_W)
    head_of_ch = jnp.arange(SSD_INNER) // SSD_HEAD_DIM
    expand = (jnp.arange(LANE)[:, None] == head_of_ch[None, :]).astype(BF16)
    expand3 = jnp.tile(expand, (3, 1))
    head_of_row = jnp.arange(SSD_HEADS_PER_GROUP * SSD_T) // SSD_T
    hmask = (head_of_row[:, None] == head_of_ch[None, :SSD_GROUP_W]).astype(BF16)
    return pl.pallas_call(
        functools.partial(_ssd_kernel, dims.pad),
        grid=(dims.nb, dims.lp // SSD_T),
        in_specs=[
            pl.BlockSpec((None, SSD_T, XBC_WIDTH), lambda b, c: (b, c, 0)),
            pl.BlockSpec((None, SSD_T, SSD_INNER), lambda b, c: (b, c, z_blk)),
            pl.BlockSpec((None, SSD_T, LANE), lambda b, c: (b, c, 0)),
            pl.BlockSpec((1, LANE), c2),
            pl.BlockSpec((1, LANE), c2),
            pl.BlockSpec((3 * LANE, SSD_INNER), c2),
            pl.BlockSpec((SSD_HEADS_PER_GROUP * SSD_T, SSD_GROUP_W), c2),
            pl.BlockSpec((1, SSD_INNER), c2),
            pl.BlockSpec((1, SSD_INNER), c2),
            pl.BlockSpec(st_shape, lambda b, c: (0, 0, 0)),
        ],
        out_specs=[
            pl.BlockSpec((None, SSD_T, SSD_INNER), lambda b, c: (b, c, 0)),
            pl.BlockSpec((None,) + st_shape, lambda b, c: (b, 0, 0, 0)),
        ],
        out_shape=[
            jax.ShapeDtypeStruct((dims.nb, dims.lp, SSD_INNER), BF16),
            jax.ShapeDtypeStruct((dims.nb,) + st_shape, F32),
        ],
        scratch_shapes=[pltpu.VMEM(st_shape, F32)],
        compiler_params=_params(2),
        name="ssd",
    )(act3, oth3, dt3, dt_bias, a_log, expand3, hmask, d_full, norm_w, state_in)


def _s5_disc_kernel(are_ref, aim_ref, ls_ref, bre_ref, bim_ref, abre_ref, abim_ref, bbre_ref, bbim_ref):
    step = jnp.exp(ls_ref[...])
    lam_re = are_ref[...]
    lam_im = aim_ref[...]
    mag = jnp.exp(lam_re * step)
    ab_re = mag * jnp.cos(lam_im * step)
    ab_im = mag * jnp.sin(lam_im * step)
    den = lam_re * lam_re + lam_im * lam_im
    nr = ab_re - 1.0
    f_re = (nr * lam_re + ab_im * lam_im) / den
    f_im = (ab_im * lam_re - nr * lam_im) / den
    br = bre_ref[...]
    bi = bim_ref[...]
    abre_ref[...] = ab_re
    abim_ref[...] = ab_im
    bbre_ref[...] = f_re * br - f_im * bi
    bbim_ref[...] = f_re * bi + f_im * br


def _s5_disc(a_re, a_im, log_step, b_re_t, b_im_t):
    sds = jax.ShapeDtypeStruct
    return pl.pallas_call(
        _s5_disc_kernel,
        out_shape=[sds(a_re.shape, F32), sds(a_re.shape, F32), sds(b_re_t.shape, F32), sds(b_re_t.shape, F32)],
        name="s5_disc",
    )(a_re, a_im, log_step, b_re_t, b_im_t)


def _s5_kernel(nb, u_ref, bre_ref, bim_ref, are_ref, aim_ref, cre_ref, cim_ref, hin_ref,
               y_ref, hout_ref, uslab_ref, yslab_ref, bu_ref, h_ref):
    @pl.when(pl.program_id(0) == 0)
    def _():
        h_ref[...] = jnp.broadcast_to(hin_ref[...], h_ref.shape)

    blk_rows = nb * SUBLANE

    u32 = u_ref[...].astype(F32)
    for q in range(S5_SUB):
        uq = u32[:, q * SUBLANE:(q + 1) * SUBLANE, :].reshape(blk_rows, S5_WIDTH)
        for k in range(S5_USLABS):
            uslab_ref[q * S5_USLABS + k] = uq[:, k * LANE:(k + 1) * LANE]
    steps = []
    for t in range(S5_TC):
        q, tl = divmod(t, SUBLANE)
        rows = pl.ds(tl, nb, stride=SUBLANE)
        steps.append(jnp.concatenate(
            [uslab_ref[q * S5_USLABS + k, rows, :] for k in range(S5_USLABS)], axis=1))
    u = jnp.concatenate(steps, axis=0).astype(BF16)

    n_half = bre_ref.shape[0]
    kw = S5_WIDTH // n_half
    nw = S5_LANES // n_half
    pieces = []
    for j in range(S5_LANES // S5_CTILE):
        k, off = divmod(j * S5_CTILE, nw)
        ks = slice(j * S5_CTILE, (j + 1) * S5_CTILE)
        ki = slice(S5_LANES + j * S5_CTILE, S5_LANES + (j + 1) * S5_CTILE)
        uk = u[:, k * kw:(k + 1) * kw]
        bu_ref[:, ks] = _bdot(uk, bre_ref[k, :, off:off + S5_CTILE])
        bu_ref[:, ki] = _bdot(uk, bim_ref[k, :, off:off + S5_CTILE])
        for s in range(j * S5_CTILE // LANE, (j + 1) * S5_CTILE // LANE):
            ls = slice(s * LANE, (s + 1) * LANE)
            li = slice(S5_LANES + s * LANE, S5_LANES + (s + 1) * LANE)
            a_r = are_ref[:, ls]
            a_i = aim_ref[:, ls]
            h_r = h_ref[:, ls]
            h_i = h_ref[:, li]
            for t in range(S5_TC):
                rows = slice(t * nb, (t + 1) * nb)
                n_r = a_r * h_r - a_i * h_i + bu_ref[rows, ls]
                n_i = a_r * h_i + a_i * h_r + bu_ref[rows, li]
                h_r, h_i = n_r, n_i
                bu_ref[rows, ls] = h_r
                bu_ref[rows, li] = h_i
            h_ref[:, ls] = h_r
            h_ref[:, li] = h_i
        pieces.append(_bdot(bu_ref[:, ks].astype(BF16), cre_ref[j])
                      - _bdot(bu_ref[:, ki].astype(BF16), cim_ref[j]))
    y = jnp.concatenate(pieces, axis=1)

    for t in range(S5_TC):
        q, tl = divmod(t, SUBLANE)
        rows = pl.ds(tl, nb, stride=SUBLANE)
        for k in range(S5_USLABS):
            yslab_ref[q * S5_USLABS + k, rows, :] = y[t * nb:(t + 1) * nb, k * LANE:(k + 1) * LANE]
    halves = []
    for q in range(S5_SUB):
        yq = jnp.concatenate([yslab_ref[q * S5_USLABS + k] for k in range(S5_USLABS)], axis=1)
        halves.append(yq.reshape(nb, SUBLANE, S5_WIDTH))
    y_ref[...] = jnp.concatenate(halves, axis=1).astype(BF16)

    @pl.when(pl.program_id(0) == pl.num_programs(0) - 1)
    def _():
        hout_ref[...] = h_ref[...]


def _s5(dims, oth3, bre_bd, bim_bd, ab_re, ab_im, cre_bd, cim_bd, state_in):
    nb = dims.nb
    u_blk = (OTH_W - S5_WIDTH) // S5_WIDTH
    c2 = lambda i: (0, 0)
    c3 = lambda i: (0, 0, 0)
    return pl.pallas_call(
        functools.partial(_s5_kernel, nb),
        grid=(dims.lp // S5_TC,),
        in_specs=[
            pl.BlockSpec((nb, S5_TC, S5_WIDTH), lambda i: (0, i, u_blk)),
            pl.BlockSpec(bre_bd.shape, c3),
            pl.BlockSpec(bim_bd.shape, c3),
            pl.BlockSpec((1, S5_LANES), c2),
            pl.BlockSpec((1, S5_LANES), c2),
            pl.BlockSpec(cre_bd.shape, c3),
            pl.BlockSpec(cim_bd.shape, c3),
            pl.BlockSpec((1, 2 * S5_LANES), c2),
        ],
        out_specs=[
            pl.BlockSpec((nb, S5_TC, S5_WIDTH), lambda i: (0, i, 0)),
            pl.BlockSpec((nb, 2 * S5_LANES), c2),
        ],
        out_shape=[
            jax.ShapeDtypeStruct((nb, dims.lp, S5_WIDTH), BF16),
            jax.ShapeDtypeStruct((nb, 2 * S5_LANES), F32),
        ],
        scratch_shapes=[
            pltpu.VMEM((S5_SUB * S5_USLABS, nb * SUBLANE, LANE), F32),
            pltpu.VMEM((S5_SUB * S5_USLABS, nb * SUBLANE, LANE), F32),
            pltpu.VMEM((nb * S5_TC, 2 * S5_LANES), F32),
            pltpu.VMEM((nb, 2 * S5_LANES), F32),
        ],
        compiler_params=_params(1),
        name="s5",
    )(oth3, bre_bd, bim_bd, ab_re, ab_im, cre_bd, cim_bd, state_in)


def _merge_kernel(g_ref, ya_ref, ys_ref, u_ref, h_ref, sd_ref, wglu_ref, wout_ref, o_ref):
    ys = ys_ref[...].astype(F32) + sd_ref[...] * u_ref[...].astype(F32)
    cdf = 0.5 * (1.0 + jnp.tanh(math.sqrt(2.0 / math.pi) * (ys + 0.044715 * (ys * ys * ys))))
    vg = _bdot((ys * cdf).astype(BF16), wglu_ref[...])
    yb = vg[:, :D_MODEL] * _sigmoid(vg[:, D_MODEL:])
    g = g_ref[...].astype(F32)
    m = _sigmoid(g[:, :D_MODEL]) * ya_ref[...].astype(F32) + _sigmoid(g[:, D_MODEL:]) * yb
    o_ref[...] = h_ref[...] + _bdot(m.astype(BF16), wout_ref[...])


def _merge(oth, y_a, y_s5, h, s5_d, w_glu, w_out):
    rows = h.shape[0]
    tm = min(TM_MERGE, rows)
    u_blk = (OTH_W - S5_WIDTH) // S5_WIDTH
    const = lambda i: (0, 0)
    return pl.pallas_call(
        _merge_kernel,
        grid=(rows // tm,),
        in_specs=[
            pl.BlockSpec((tm, 2 * D_MODEL), lambda i: (i, 0)),
            pl.BlockSpec((tm, SSD_INNER), lambda i: (i, 0)),
            pl.BlockSpec((tm, S5_WIDTH), lambda i: (i, 0)),
            pl.BlockSpec((tm, S5_WIDTH), lambda i: (i, u_blk)),
            pl.BlockSpec((tm, D_MODEL), lambda i: (i, 0)),
            pl.BlockSpec((1, S5_WIDTH), const),
            pl.BlockSpec((S5_WIDTH, 2 * D_MODEL), const),
            pl.BlockSpec((D_MODEL, D_MODEL), const),
        ],
        out_specs=pl.BlockSpec((tm, D_MODEL), lambda i: (i, 0)),
        out_shape=jax.ShapeDtypeStruct((rows, D_MODEL), F32),
        compiler_params=_params(1),
        name="merge",
    )(oth, y_a, y_s5, oth, h, s5_d, w_glu, w_out)


def _mlp_kernel(final, h_ref, nw_ref, w1_ref, w2_ref, fw_ref, o_ref, xn_ref):
    f = pl.program_id(1)

    @pl.when(f == 0)
    def _():
        x = h_ref[...]
        xn_ref[...] = (_rms_scale(x) * nw_ref[...]).astype(BF16)
        o_ref[...] = x

    hid = jnp.square(jnp.maximum(_bdot(xn_ref[...], w1_ref[...]), 0.0))
    o_ref[...] += _bdot(hid.astype(BF16), w2_ref[...])

    if final:
        @pl.when(f == pl.num_programs(1) - 1)
        def _():
            o_ref[...] = _rms_scale(o_ref[...]) * fw_ref[...]


def _mlp(h, norm_w, w1, w2, final_w, final):
    rows = h.shape[0]
    tm = min(TM_TOK, rows)
    return pl.pallas_call(
        functools.partial(_mlp_kernel, final),
        grid=(rows // tm, D_FF // TF_MLP),
        in_specs=[
            pl.BlockSpec((tm, D_MODEL), lambda i, f: (i, 0)),
            pl.BlockSpec((1, D_MODEL), lambda i, f: (0, 0)),
            pl.BlockSpec((D_MODEL, TF_MLP), lambda i, f: (0, f)),
            pl.BlockSpec((TF_MLP, D_MODEL), lambda i, f: (f, 0)),
            pl.BlockSpec((1, D_MODEL), lambda i, f: (0, 0)),
        ],
        out_specs=pl.BlockSpec((tm, D_MODEL), lambda i, f: (i, 0)),
        out_shape=jax.ShapeDtypeStruct((rows, D_MODEL), F32),
        scratch_shapes=[pltpu.VMEM((tm, D_MODEL), BF16)],
        compiler_params=_params(2),
        name="mlp",
    )(h, norm_w, w1, w2, final_w)


def _pad_lanes(v):
    return jnp.pad(v, (0, LANE - v.shape[0])).reshape(1, LANE)


def _block_diag(blocks):
    n, r, c = blocks.shape
    eye = jnp.eye(n, dtype=blocks.dtype)
    return (blocks[:, :, None, :] * eye[:, None, :, None]).reshape(n * r, n * c)


def _mixers(dims, h, p, conv_tail, ssd_state, s5_state):
    act3, oth3, dt3, tail = _inproj(
        dims, h.reshape(dims.nb, dims.lp, D_MODEL), p["norm_mix_w"], p["w_x"], p["w_o"], p["w_dt"],
        p["conv_w"], p["conv_b"], conv_tail)
    y_a, st = _ssd(dims, act3, oth3, dt3, p["dt_bias"], p["a_log"], p["d_full"], p["ssd_norm_w"],
                   ssd_state)
    s5_dims = dims._replace(nb=max(dims.nb, SUBLANE))
    y_s5, hs = _s5(s5_dims, jnp.broadcast_to(oth3, (s5_dims.nb,) + oth3.shape[1:]),
                   p["bre_bd"], p["bim_bd"], p["ab_re"], p["ab_im"], p["cre_bd"], p["cim_bd"], s5_state)
    y_s5 = y_s5[:dims.nb]
    rows = dims.nb * dims.lp
    h = _merge(oth3.reshape(rows, OTH_W), y_a.reshape(rows, SSD_INNER), y_s5.reshape(rows, S5_WIDTH),
               h, p["s5_d"], p["w_glu"], p["w_out"])
    return h, tail, st, hs


def kernel(x, meta_tokens, w_in, conv_w, conv_b, dt_bias, ssd_a_log, ssd_d, ssd_norm_w, s5_a_re, s5_a_im, s5_log_step, s5_b_re, s5_b_im, s5_c_re, s5_c_im, s5_d, w_glu, w_out, norm_mix_w, norm_mlp_w, w_ff_in, w_ff_out, final_norm_w):
    o_xbc = SSD_INNER
    o_dt = o_xbc + XBC_WIDTH
    o_u = o_dt + SSD_HEADS
    o_g = o_u + S5_WIDTH
    n_half = 2
    gph = S5_GROUPS // n_half
    gpc = S5_CTILE // S5_STATE
    final_w = final_norm_w.reshape(1, D_MODEL)

    h = x.reshape(BATCH * SEQ, D_MODEL)
    meta_seq = jnp.concatenate(
        [jnp.zeros((PREFIX.pad, D_MODEL), x.dtype), meta_tokens.astype(x.dtype)], axis=0)
    h_pre = jnp.broadcast_to(meta_seq[None], (PREFIX.nb, PREFIX.lp, D_MODEL)).reshape(-1, D_MODEL)
    zero_tail = jnp.zeros((SUBLANE, XBC_WIDTH), F32)
    zero_ssd = jnp.zeros((SSD_GROUPS, SSD_STATE, SSD_GROUP_W), F32)
    zero_s5 = jnp.zeros((1, 2 * S5_LANES), F32)

    for i in range(DEPTH):
        wi = w_in[i]
        ab_re, ab_im, bb_re_t, bb_im_t = _s5_disc(
            s5_a_re[i].reshape(S5_GROUPS, 1, S5_STATE), s5_a_im[i].reshape(S5_GROUPS, 1, S5_STATE),
            s5_log_step[i].reshape(S5_GROUPS, 1, 1),
            jnp.swapaxes(s5_b_re[i], 1, 2), jnp.swapaxes(s5_b_im[i], 1, 2))
        bd_b = lambda bt: jnp.stack(
            [_block_diag(bt[k * gph:(k + 1) * gph]) for k in range(n_half)]).astype(BF16)
        bd_c = lambda cm: jnp.stack(
            [_block_diag(jnp.swapaxes(cm, 1, 2)[j * gpc:(j + 1) * gpc])
             for j in range(S5_GROUPS // gpc)]).astype(BF16)
        p = {
            "norm_mix_w": norm_mix_w[i].reshape(1, D_MODEL),
            "w_x": wi[:, o_xbc:o_dt].astype(BF16),
            "w_o": jnp.concatenate([wi[:, o_g:], wi[:, :o_xbc], wi[:, o_u:o_g]], axis=1).astype(BF16),
            "w_dt": jnp.pad(wi[:, o_dt:o_u], ((0, 0), (0, LANE - SSD_HEADS))).astype(BF16),
            "conv_w": conv_w[i],
            "conv_b": conv_b[i].reshape(1, XBC_WIDTH),
            "dt_bias": _pad_lanes(dt_bias[i]),
            "a_log": _pad_lanes(ssd_a_log[i]),
            "d_full": jnp.repeat(ssd_d[i], SSD_HEAD_DIM).reshape(1, SSD_INNER),
            "ssd_norm_w": ssd_norm_w[i].reshape(1, SSD_INNER),
            "bre_bd": bd_b(bb_re_t),
            "bim_bd": bd_b(bb_im_t),
            "ab_re": ab_re.reshape(1, S5_LANES),
            "ab_im": ab_im.reshape(1, S5_LANES),
            "cre_bd": bd_c(s5_c_re[i]),
            "cim_bd": bd_c(s5_c_im[i]),
            "s5_d": s5_d[i].reshape(1, S5_WIDTH),
            "w_glu": w_glu[i].astype(BF16),
            "w_out": w_out[i].astype(BF16),
        }
        mlp_w = (norm_mlp_w[i].reshape(1, D_MODEL), w_ff_in[i].astype(BF16), w_ff_out[i].astype(BF16))
        last = i == DEPTH - 1

        h_pre, tail, st, hs = _mixers(PREFIX, h_pre, p, zero_tail, zero_ssd, zero_s5)
        if not last:
            h_pre = _mlp(h_pre, *mlp_w, final_w, False)
        h, _, _, _ = _mixers(MAIN, h, p, tail[0, -1], st[0], hs[0:1])
        h = _mlp(h, *mlp_w, final_w, last)

    return h.reshape(BATCH, SEQ, D_MODEL)
```

```python
import functools
import math
from typing import NamedTuple

import jax
import jax.numpy as jnp
from jax import lax
from jax.experimental import pallas as pl
from jax.experimental.pallas import tpu as pltpu

F32 = jnp.float32
BF16 = jnp.bfloat16

D_MODEL = 1024
BATCH = 32
SEQ = 2048
DEPTH = 2
N_META = 16
SSD_HEADS = 16
SSD_HEAD_DIM = 64
SSD_INNER = SSD_HEADS * SSD_HEAD_DIM
SSD_GROUPS = 4
SSD_HEADS_PER_GROUP = SSD_HEADS // SSD_GROUPS
SSD_GROUP_W = SSD_INNER // SSD_GROUPS
SSD_STATE = 128
SSD_CONV = 4
S5_WIDTH = D_MODEL // 2
S5_GROUP = 16
S5_GROUPS = S5_WIDTH // S5_GROUP
S5_STATE = 64
S5_LANES = S5_GROUPS * S5_STATE
D_FF = 4 * D_MODEL
EPS = 1e-6
XBC_WIDTH = SSD_INNER + 2 * SSD_GROUPS * SSD_STATE

LANE = 128
SUBLANE = 8

SSD_T = 128
S5_SUB = 2
S5_TC = S5_SUB * SUBLANE
S5_CTILE = 256
S5_USLABS = S5_WIDTH // LANE

OTH_W = 2 * D_MODEL + SSD_INNER + S5_WIDTH
NJ_PROJ = 4
TN_ACT = XBC_WIDTH // NJ_PROJ
TN_OTH = OTH_W // NJ_PROJ
TM_PROJ = 2048
PROJ_CHUNK = 256
CONV_ROWS = 64
TM_TOK = 1024
TM_MERGE = 512
TF_MLP = 2048
VMEM_LIMIT = 56 * 1024 * 1024


class Dims(NamedTuple):
    nb: int
    lp: int
    pad: int


MAIN = Dims(BATCH, SEQ, 0)
PREFIX = Dims(1, SSD_T, SSD_T - N_META)


def _sigmoid(x):
    return 0.5 * (jnp.tanh(0.5 * x) + 1.0)


def _rms_scale(x):
    return x * lax.rsqrt(jnp.mean(x * x, axis=-1, keepdims=True) + EPS)


def _bdot(a, b):
    return jnp.dot(a, b, preferred_element_type=F32)


def _params(n_axes):
    return pltpu.CompilerParams(
        dimension_semantics=("arbitrary",) * n_axes, vmem_limit_bytes=VMEM_LIMIT)


def _inproj_kernel(tm, h_ref, nw_ref, wx_ref, wo_ref, wdt_ref, cw_ref, cb_ref, cin_ref,
                   act_ref, oth_ref, dt_ref, tail_ref, xn_ref, ext_ref, carry_ref):
    i = pl.program_id(1)
    j = pl.program_id(2)

    @pl.when(j == 0)
    def _():
        xn = (_rms_scale(h_ref[...]) * nw_ref[...]).astype(BF16)
        xn_ref[...] = xn
        dt_ref[...] = _bdot(xn, wdt_ref[...])

    def project(lo, hi):
        xn = xn_ref[lo:hi, :]
        ext_ref[SUBLANE + lo:SUBLANE + hi, :] = _bdot(xn, wx_ref[...])
        oth_ref[lo:hi, :] = _bdot(xn, wo_ref[...]).astype(BF16)

    def conv_silu(lo, hi):
        for r0 in range(lo, hi, CONV_ROWS):
            for lb in range(TN_ACT // LANE):
                ls = slice(lb * LANE, (lb + 1) * LANE)
                xe = ext_ref[r0:r0 + SUBLANE + CONV_ROWS, ls]
                acc = cb_ref[:, ls] + cw_ref[SSD_CONV - 1:SSD_CONV, ls] * xe[SUBLANE:, :]
                for k in range(SSD_CONV - 1):
                    shifted = pltpu.roll(xe, SSD_CONV - 1 - k, 0)[SUBLANE:, :]
                    acc = acc + cw_ref[k:k + 1, ls] * shifted
                half = 0.5 * acc
                act_ref[r0:r0 + CONV_ROWS, ls] = (half * (jnp.tanh(half) + 1.0)).astype(BF16)

    ext_ref[0:SUBLANE, :] = jnp.where(i == 0, cin_ref[...], carry_ref[j])
    bounds = tuple(range(0, tm, PROJ_CHUNK)) + (tm,)
    project(bounds[0], bounds[1])
    for q in range(1, len(bounds) - 1):
        conv_silu(bounds[q - 1], bounds[q])
        project(bounds[q], bounds[q + 1])
    conv_silu(bounds[-2], bounds[-1])
    tail = ext_ref[tm:tm + SUBLANE, :]
    carry_ref[j] = tail
    tail_ref[...] = tail


def _inproj(dims, h3, norm_w, w_x, w_o, w_dt, conv_w, conv_b, conv_tail):
    tm = min(TM_PROJ, dims.lp)
    c2 = lambda b, i, j: (0, 0)
    sds = jax.ShapeDtypeStruct
    return pl.pallas_call(
        functools.partial(_inproj_kernel, tm),
        grid=(dims.nb, dims.lp // tm, NJ_PROJ),
        in_specs=[
            pl.BlockSpec((None, tm, D_MODEL), lambda b, i, j: (b, i, 0)),
            pl.BlockSpec((1, D_MODEL), c2),
            pl.BlockSpec((D_MODEL, TN_ACT), lambda b, i, j: (0, j)),
            pl.BlockSpec((D_MODEL, TN_OTH), lambda b, i, j: (0, j)),
            pl.BlockSpec((D_MODEL, LANE), c2),
            pl.BlockSpec((SSD_CONV, TN_ACT), lambda b, i, j: (0, j)),
            pl.BlockSpec((1, TN_ACT), lambda b, i, j: (0, j)),
            pl.BlockSpec((SUBLANE, TN_ACT), lambda b, i, j: (0, j)),
        ],
        out_specs=[
            pl.BlockSpec((None, tm, TN_ACT), lambda b, i, j: (b, i, j)),
            pl.BlockSpec((None, tm, TN_OTH), lambda b, i, j: (b, i, j)),
            pl.BlockSpec((None, tm, LANE), lambda b, i, j: (b, i, 0)),
            pl.BlockSpec((None, None, SUBLANE, TN_ACT), lambda b, i, j: (b, i, 0, j)),
        ],
        out_shape=[
            sds((dims.nb, dims.lp, XBC_WIDTH), BF16),
            sds((dims.nb, dims.lp, OTH_W), BF16),
            sds((dims.nb, dims.lp, LANE), F32),
            sds((dims.nb, dims.lp // tm, SUBLANE, XBC_WIDTH), F32),
        ],
        scratch_shapes=[
            pltpu.VMEM((tm, D_MODEL), BF16),
            pltpu.VMEM((tm + SUBLANE, TN_ACT), F32),
            pltpu.VMEM((NJ_PROJ, SUBLANE, TN_ACT), F32),
        ],
        compiler_params=_params(3),
        name="inproj",
    )(h3, norm_w, w_x, w_o, w_dt, conv_w, conv_b, conv_tail)


def _split_bf16(x, n, axis):
    parts = []
    for _ in range(n):
        p = x.astype(BF16)
        parts.append(p)
        x = x - p.astype(F32)
    return jnp.concatenate(parts, axis=axis)


def _ssd_kernel(pad, act_ref, z_ref, dt_ref, dtb_ref, alog_ref, exp_ref, hmask_ref, dsk_ref, nw_ref,
                stin_ref, y_ref, stout_ref, st_ref):
    c = pl.program_id(1)
    t = SSD_T

    @pl.when(c == 0)
    def _():
        st_ref[...] = stin_ref[...]

    xdt = dt_ref[...] + dtb_ref[...]
    dt = jnp.maximum(xdt, 0.0) + jnp.log1p(jnp.exp(-jnp.abs(xdt)))
    if pad:
        row = c * t + lax.broadcasted_iota(jnp.int32, (t, LANE), 0)
        dt = jnp.where(row >= pad, dt, 0.0)
    a = -jnp.exp(alog_ref[...])
    r_i = lax.broadcasted_iota(jnp.int32, (t, t), 0)
    c_i = lax.broadcasted_iota(jnp.int32, (t, t), 1)
    causal = r_i >= c_i
    tri = causal.astype(BF16)
    cs = _bdot(jnp.concatenate([tri, tri, tri], axis=1), _split_bf16(dt * a, 3, 0))
    cs_t = cs.T

    dt_ch = _bdot(_split_bf16(dt, 2, 1), exp_ref[0:2 * LANE, :])
    cs_ch = _bdot(_split_bf16(cs, 3, 1), exp_ref[...])
    cs_end = cs_ch[t - 1:t, :]
    xs = act_ref[:, :SSD_INNER].astype(F32)
    xdt = xs * dt_ch
    xdt_b = xdt.astype(BF16)
    xdt_end_b = (xdt * jnp.exp(cs_end - cs_ch)).astype(BF16)

    for g in range(SSD_GROUPS):
        gs = slice(g * SSD_GROUP_W, (g + 1) * SSD_GROUP_W)
        b_lo = SSD_INNER + g * SSD_STATE
        c_lo = SSD_INNER + SSD_GROUPS * SSD_STATE + g * SSD_STATE
        b_gt = act_ref[:, b_lo:b_lo + SSD_STATE].astype(F32).T.astype(BF16)
        c_g = act_ref[:, c_lo:c_lo + SSD_STATE]
        scores = _bdot(c_g, b_gt)
        m_parts = []
        for r in range(SSD_HEADS_PER_GROUP):
            h = g * SSD_HEADS_PER_GROUP + r
            seg = cs[:, h:h + 1] - cs_t[h:h + 1, :]
            decay = jnp.exp(jnp.where(causal, seg, -jnp.inf))
            m_parts.append((scores * decay).astype(BF16))
        x_bd = jnp.concatenate([xdt_b[:, gs]] * SSD_HEADS_PER_GROUP, axis=0) * hmask_ref[...]
        y = _bdot(jnp.concatenate(m_parts, axis=1), x_bd)
        st = st_ref[g]
        y = y + _bdot(c_g, st.astype(BF16)) * jnp.exp(cs_ch[:, gs])
        st_ref[g] = st * jnp.exp(cs_end[:, gs]) + _bdot(b_gt, xdt_end_b[:, gs])

        z = z_ref[:, gs].astype(F32)
        y = (y + dsk_ref[:, gs] * xs[:, gs]) * (z * _sigmoid(z))
        y_ref[:, gs] = (_rms_scale(y) * nw_ref[:, gs]).astype(BF16)

    @pl.when(c == pl.num_programs(1) - 1)
    def _():
        stout_ref[...] = st_ref[...]


def _ssd(dims, act3, oth3, dt3, dt_bias, a_log, d_full, norm_w, state_in):
    z_blk = 2 * D_MODEL // SSD_INNER
    c2 = lambda b, c: (0, 0)
    st_shape = (SSD_GROUPS, SSD_STATE, SSD_GROUP_W)
    head_of_ch = jnp.arange(SSD_INNER) // SSD_HEAD_DIM
    expand = (jnp.arange(LANE)[:, None] == head_of_ch[None, :]).astype(BF16)
    expand3 = jnp.tile(expand, (3, 1))
    head_of_row = jnp.arange(SSD_HEADS_PER_GROUP * SSD_T) // SSD_T
    hmask = (head_of_row[:, None] == head_of_ch[None, :SSD_GROUP_W]).astype(BF16)
    return pl.pallas_call(
        functools.partial(_ssd_kernel, dims.pad),
        grid=(dims.nb, dims.lp // SSD_T),
        in_specs=[
            pl.BlockSpec((None, SSD_T, XBC_WIDTH), lambda b, c: (b, c, 0)),
            pl.BlockSpec((None, SSD_T, SSD_INNER), lambda b, c: (b, c, z_blk)),
            pl.BlockSpec((None, SSD_T, LANE), lambda b, c: (b, c, 0)),
            pl.BlockSpec((1, LANE), c2),
            pl.BlockSpec((1, LANE), c2),
            pl.BlockSpec((3 * LANE, SSD_INNER), c2),
            pl.BlockSpec((SSD_HEADS_PER_GROUP * SSD_T, SSD_GROUP_W), c2),
            pl.BlockSpec((1, SSD_INNER), c2),
            pl.BlockSpec((1, SSD_INNER), c2),
            pl.BlockSpec(st_shape, lambda b, c: (0, 0, 0)),
        ],
        out_specs=[
            pl.BlockSpec((None, SSD_T, SSD_INNER), lambda b, c: (b, c, 0)),
            pl.BlockSpec((None,) + st_shape, lambda b, c: (b, 0, 0, 0)),
        ],
        out_shape=[
            jax.ShapeDtypeStruct((dims.nb, dims.lp, SSD_INNER), BF16),
            jax.ShapeDtypeStruct((dims.nb,) + st_shape, F32),
        ],
        scratch_shapes=[pltpu.VMEM(st_shape, F32)],
        compiler_params=_params(2),
        name="ssd",
    )(act3, oth3, dt3, dt_bias, a_log, expand3, hmask, d_full, norm_w, state_in)


def _s5_disc_kernel(are_ref, aim_ref, ls_ref, bre_ref, bim_ref, abre_ref, abim_ref, bbre_ref, bbim_ref):
    step = jnp.exp(ls_ref[...])
    lam_re = are_ref[...]
    lam_im = aim_ref[...]
    mag = jnp.exp(lam_re * step)
    ab_re = mag * jnp.cos(lam_im * step)
    ab_im = mag * jnp.sin(lam_im * step)
    den = lam_re * lam_re + lam_im * lam_im
    nr = ab_re - 1.0
    f_re = (nr * lam_re + ab_im * lam_im) / den
    f_im = (ab_im * lam_re - nr * lam_im) / den
    br = bre_ref[...]
    bi = bim_ref[...]
    abre_ref[...] = ab_re
    abim_ref[...] = ab_im
    bbre_ref[...] = f_re * br - f_im * bi
    bbim_ref[...] = f_re * bi + f_im * br


def _s5_disc(a_re, a_im, log_step, b_re_t, b_im_t):
    sds = jax.ShapeDtypeStruct
    return pl.pallas_call(
        _s5_disc_kernel,
        out_shape=[sds(a_re.shape, F32), sds(a_re.shape, F32), sds(b_re_t.shape, F32), sds(b_re_t.shape, F32)],
        name="s5_disc",
    )(a_re, a_im, log_step, b_re_t, b_im_t)


def _s5_kernel(nb, u_ref, bre_ref, bim_ref, are_ref, aim_ref, cre_ref, cim_ref, hin_ref,
               y_ref, hout_ref, uslab_ref, yslab_ref, bu_ref, h_ref):
    @pl.when(pl.program_id(0) == 0)
    def _():
        h_ref[...] = jnp.broadcast_to(hin_ref[...], h_ref.shape)

    blk_rows = nb * SUBLANE

    u32 = u_ref[...].astype(F32)
    for q in range(S5_SUB):
        uq = u32[:, q * SUBLANE:(q + 1) * SUBLANE, :].reshape(blk_rows, S5_WIDTH)
        for k in range(S5_USLABS):
            uslab_ref[q * S5_USLABS + k] = uq[:, k * LANE:(k + 1) * LANE]
    steps = []
    for t in range(S5_TC):
        q, tl = divmod(t, SUBLANE)
        rows = pl.ds(tl, nb, stride=SUBLANE)
        steps.append(jnp.concatenate(
            [uslab_ref[q * S5_USLABS + k, rows, :] for k in range(S5_USLABS)], axis=1))
    u = jnp.concatenate(steps, axis=0).astype(BF16)

    n_half = bre_ref.shape[0]
    kw = S5_WIDTH // n_half
    nw = S5_LANES // n_half
    pieces = []
    for j in range(S5_LANES // S5_CTILE):
        k, off = divmod(j * S5_CTILE, nw)
        ks = slice(j * S5_CTILE, (j + 1) * S5_CTILE)
        ki = slice(S5_LANES + j * S5_CTILE, S5_LANES + (j + 1) * S5_CTILE)
        uk = u[:, k * kw:(k + 1) * kw]
        bu_ref[:, ks] = _bdot(uk, bre_ref[k, :, off:off + S5_CTILE])
        bu_ref[:, ki] = _bdot(uk, bim_ref[k, :, off:off + S5_CTILE])
        for s in range(j * S5_CTILE // LANE, (j + 1) * S5_CTILE // LANE):
            ls = slice(s * LANE, (s + 1) * LANE)
            li = slice(S5_LANES + s * LANE, S5_LANES + (s + 1) * LANE)
            a_r = are_ref[:, ls]
            a_i = aim_ref[:, ls]
            h_r = h_ref[:, ls]
            h_i = h_ref[:, li]
            for t in range(S5_TC):
                rows = slice(t * nb, (t + 1) * nb)
                n_r = a_r * h_r - a_i * h_i + bu_ref[rows, ls]
                n_i = a_r * h_i + a_i * h_r + bu_ref[rows, li]
                h_r, h_i = n_r, n_i
                bu_ref[rows, ls] = h_r
                bu_ref[rows, li] = h_i
            h_ref[:, ls] = h_r
            h_ref[:, li] = h_i
        pieces.append(_bdot(bu_ref[:, ks].astype(BF16), cre_ref[j])
                      - _bdot(bu_ref[:, ki].astype(BF16), cim_ref[j]))
    y = jnp.concatenate(pieces, axis=1)

    for t in range(S5_TC):
        q, tl = divmod(t, SUBLANE)
        rows = pl.ds(tl, nb, stride=SUBLANE)
        for k in range(S5_USLABS):
            yslab_ref[q * S5_USLABS + k, rows, :] = y[t * nb:(t + 1) * nb, k * LANE:(k + 1) * LANE]
    halves = []
    for q in range(S5_SUB):
        yq = jnp.concatenate([yslab_ref[q * S5_USLABS + k] for k in range(S5_USLABS)], axis=1)
        halves.append(yq.reshape(nb, SUBLANE, S5_WIDTH))
    y_ref[...] = jnp.concatenate(halves, axis=1).astype(BF16)

    @pl.when(pl.program_id(0) == pl.num_programs(0) - 1)
    def _():
        hout_ref[...] = h_ref[...]


def _s5(dims, oth3, bre_bd, bim_bd, ab_re, ab_im, cre_bd, cim_bd, state_in):
    nb = dims.nb
    u_blk = (OTH_W - S5_WIDTH) // S5_WIDTH
    c2 = lambda i: (0, 0)
    c3 = lambda i: (0, 0, 0)
    return pl.pallas_call(
        functools.partial(_s5_kernel, nb),
        grid=(dims.lp // S5_TC,),
        in_specs=[
            pl.BlockSpec((nb, S5_TC, S5_WIDTH), lambda i: (0, i, u_blk)),
            pl.BlockSpec(bre_bd.shape, c3),
            pl.BlockSpec(bim_bd.shape, c3),
            pl.BlockSpec((1, S5_LANES), c2),
            pl.BlockSpec((1, S5_LANES), c2),
            pl.BlockSpec(cre_bd.shape, c3),
            pl.BlockSpec(cim_bd.shape, c3),
            pl.BlockSpec((1, 2 * S5_LANES), c2),
        ],
        out_specs=[
            pl.BlockSpec((nb, S5_TC, S5_WIDTH), lambda i: (0, i, 0)),
            pl.BlockSpec((nb, 2 * S5_LANES), c2),
        ],
        out_shape=[
            jax.ShapeDtypeStruct((nb, dims.lp, S5_WIDTH), BF16),
            jax.ShapeDtypeStruct((nb, 2 * S5_LANES), F32),
        ],
        scratch_shapes=[
            pltpu.VMEM((S5_SUB * S5_USLABS, nb * SUBLANE, LANE), F32),
            pltpu.VMEM((S5_SUB * S5_USLABS, nb * SUBLANE, LANE), F32),
            pltpu.VMEM((nb * S5_TC, 2 * S5_LANES), F32),
            pltpu.VMEM((nb, 2 * S5_LANES), F32),
        ],
        compiler_params=_params(1),
        name="s5",
    )(oth3, bre_bd, bim_bd, ab_re, ab_im, cre_bd, cim_bd, state_in)


def _merge_kernel(g_ref, ya_ref, ys_ref, u_ref, h_ref, sd_ref, wglu_ref, wout_ref, o_ref):
    ys = ys_ref[...].astype(F32) + sd_ref[...] * u_ref[...].astype(F32)
    cdf = 0.5 * (1.0 + jnp.tanh(math.sqrt(2.0 / math.pi) * (ys + 0.044715 * (ys * ys * ys))))
    vg = _bdot((ys * cdf).astype(BF16), wglu_ref[...])
    yb = vg[:, :D_MODEL] * _sigmoid(vg[:, D_MODEL:])
    g = g_ref[...].astype(F32)
    m = _sigmoid(g[:, :D_MODEL]) * ya_ref[...].astype(F32) + _sigmoid(g[:, D_MODEL:]) * yb
    o_ref[...] = h_ref[...] + _bdot(m.astype(BF16), wout_ref[...])


def _merge(oth, y_a, y_s5, h, s5_d, w_glu, w_out):
    rows = h.shape[0]
    tm = min(TM_MERGE, rows)
    u_blk = (OTH_W - S5_WIDTH) // S5_WIDTH
    const = lambda i: (0, 0)
    return pl.pallas_call(
        _merge_kernel,
        grid=(rows // tm,),
        in_specs=[
            pl.BlockSpec((tm, 2 * D_MODEL), lambda i: (i, 0)),
            pl.BlockSpec((tm, SSD_INNER), lambda i: (i, 0)),
            pl.BlockSpec((tm, S5_WIDTH), lambda i: (i, 0)),
            pl.BlockSpec((tm, S5_WIDTH), lambda i: (i, u_blk)),
            pl.BlockSpec((tm, D_MODEL), lambda i: (i, 0)),
            pl.BlockSpec((1, S5_WIDTH), const),
            pl.BlockSpec((S5_WIDTH, 2 * D_MODEL), const),
            pl.BlockSpec((D_MODEL, D_MODEL), const),
        ],
        out_specs=pl.BlockSpec((tm, D_MODEL), lambda i: (i, 0)),
        out_shape=jax.ShapeDtypeStruct((rows, D_MODEL), F32),
        compiler_params=_params(1),
        name="merge",
    )(oth, y_a, y_s5, oth, h, s5_d, w_glu, w_out)


def _mlp_kernel(final, h_ref, nw_ref, w1_ref, w2_ref, fw_ref, o_ref, xn_ref):
    f = pl.program_id(1)

    @pl.when(f == 0)
    def _():
        x = h_ref[...]
        xn_ref[...] = (_rms_scale(x) * nw_ref[...]).astype(BF16)
        o_ref[...] = x

    hid = jnp.square(jnp.maximum(_bdot(xn_ref[...], w1_ref[...]), 0.0))
    o_ref[...] += _bdot(hid.astype(BF16), w2_ref[...])

    if final:
        @pl.when(f == pl.num_programs(1) - 1)
        def _():
            o_ref[...] = _rms_scale(o_ref[...]) * fw_ref[...]


def _mlp(h, norm_w, w1, w2, final_w, final):
    rows = h.shape[0]
    tm = min(TM_TOK, rows)
    return pl.pallas_call(
        functools.partial(_mlp_kernel, final),
        grid=(rows // tm, D_FF // TF_MLP),
        in_specs=[
            pl.BlockSpec((tm, D_MODEL), lambda i, f: (i, 0)),
            pl.BlockSpec((1, D_MODEL), lambda i, f: (0, 0)),
            pl.BlockSpec((D_MODEL, TF_MLP), lambda i, f: (0, f)),
            pl.BlockSpec((TF_MLP, D_MODEL), lambda i, f: (f, 0)),
            pl.BlockSpec((1, D_MODEL), lambda i, f: (0, 0)),
        ],
        out_specs=pl.BlockSpec((tm, D_MODEL), lambda i, f: (i, 0)),
        out_shape=jax.ShapeDtypeStruct((rows, D_MODEL), F32),
        scratch_shapes=[pltpu.VMEM((tm, D_MODEL), BF16)],
        compiler_params=_params(2),
        name="mlp",
    )(h, norm_w, w1, w2, final_w)


def _pad_lanes(v):
    return jnp.pad(v, (0, LANE - v.shape[0])).reshape(1, LANE)


def _block_diag(blocks):
    s, n, r, c = blocks.shape
    eye = jnp.eye(n, dtype=blocks.dtype)
    return (blocks[:, :, :, None, :] * eye[None, :, None, :, None]).reshape(s, n * r, n * c)


def _mixers(dims, h, p, conv_tail, ssd_state, s5_state):
    act3, oth3, dt3, tail = _inproj(
        dims, h.reshape(dims.nb, dims.lp, D_MODEL), p["norm_mix_w"], p["w_x"], p["w_o"], p["w_dt"],
        p["conv_w"], p["conv_b"], conv_tail)
    y_a, st = _ssd(dims, act3, oth3, dt3, p["dt_bias"], p["a_log"], p["d_full"], p["ssd_norm_w"],
                   ssd_state)
    s5_dims = dims._replace(nb=max(dims.nb, SUBLANE))
    y_s5, hs = _s5(s5_dims, jnp.broadcast_to(oth3, (s5_dims.nb,) + oth3.shape[1:]),
                   p["bre_bd"], p["bim_bd"], p["ab_re"], p["ab_im"], p["cre_bd"], p["cim_bd"], s5_state)
    y_s5 = y_s5[:dims.nb]
    rows = dims.nb * dims.lp
    h = _merge(oth3.reshape(rows, OTH_W), y_a.reshape(rows, SSD_INNER), y_s5.reshape(rows, S5_WIDTH),
               h, p["s5_d"], p["w_glu"], p["w_out"])
    return h, tail, st, hs


def kernel(x, meta_tokens, w_in, conv_w, conv_b, dt_bias, ssd_a_log, ssd_d, ssd_norm_w, s5_a_re, s5_a_im, s5_log_step, s5_b_re, s5_b_im, s5_c_re, s5_c_im, s5_d, w_glu, w_out, norm_mix_w, norm_mlp_w, w_ff_in, w_ff_out, final_norm_w):
    o_xbc = SSD_INNER
    o_dt = o_xbc + XBC_WIDTH
    o_u = o_dt + SSD_HEADS
    o_g = o_u + S5_WIDTH
    n_half = 2
    gph = S5_GROUPS // n_half
    gpc = S5_CTILE // S5_STATE
    final_w = final_norm_w.reshape(1, D_MODEL)

    h = x.reshape(BATCH * SEQ, D_MODEL)
    meta_seq = jnp.concatenate(
        [jnp.zeros((PREFIX.pad, D_MODEL), x.dtype), meta_tokens.astype(x.dtype)], axis=0)
    h_pre = jnp.broadcast_to(meta_seq[None], (PREFIX.nb, PREFIX.lp, D_MODEL)).reshape(-1, D_MODEL)
    zero_tail = jnp.zeros((SUBLANE, XBC_WIDTH), F32)
    zero_ssd = jnp.zeros((SSD_GROUPS, SSD_STATE, SSD_GROUP_W), F32)
    zero_s5 = jnp.zeros((1, 2 * S5_LANES), F32)

    for i in range(DEPTH):
        wi = w_in[i]
        ab_re, ab_im, bb_re_t, bb_im_t = _s5_disc(
            s5_a_re[i].reshape(S5_GROUPS, 1, S5_STATE), s5_a_im[i].reshape(S5_GROUPS, 1, S5_STATE),
            s5_log_step[i].reshape(S5_GROUPS, 1, 1),
            jnp.swapaxes(s5_b_re[i], 1, 2), jnp.swapaxes(s5_b_im[i], 1, 2))
        bd_b = lambda bt: _block_diag(
            bt.reshape(n_half, gph, S5_GROUP, S5_STATE)).astype(BF16)
        bd_c = lambda cm: _block_diag(
            jnp.swapaxes(cm, 1, 2).reshape(S5_GROUPS // gpc, gpc, S5_STATE, S5_GROUP)).astype(BF16)
        p = {
            "norm_mix_w": norm_mix_w[i].reshape(1, D_MODEL),
            "w_x": wi[:, o_xbc:o_dt].astype(BF16),
            "w_o": jnp.concatenate([wi[:, o_g:], wi[:, :o_xbc], wi[:, o_u:o_g]], axis=1).astype(BF16),
            "w_dt": jnp.pad(wi[:, o_dt:o_u], ((0, 0), (0, LANE - SSD_HEADS))).astype(BF16),
            "conv_w": conv_w[i],
            "conv_b": conv_b[i].reshape(1, XBC_WIDTH),
            "dt_bias": _pad_lanes(dt_bias[i]),
            "a_log": _pad_lanes(ssd_a_log[i]),
            "d_full": jnp.repeat(ssd_d[i], SSD_HEAD_DIM).reshape(1, SSD_INNER),
            "ssd_norm_w": ssd_norm_w[i].reshape(1, SSD_INNER),
            "bre_bd": bd_b(bb_re_t),
            "bim_bd": bd_b(bb_im_t),
            "ab_re": ab_re.reshape(1, S5_LANES),
            "ab_im": ab_im.reshape(1, S5_LANES),
            "cre_bd": bd_c(s5_c_re[i]),
            "cim_bd": bd_c(s5_c_im[i]),
            "s5_d": s5_d[i].reshape(1, S5_WIDTH),
            "w_glu": w_glu[i].astype(BF16),
            "w_out": w_out[i].astype(BF16),
        }
        mlp_w = (norm_mlp_w[i].reshape(1, D_MODEL), w_ff_in[i].astype(BF16), w_ff_out[i].astype(BF16))
        last = i == DEPTH - 1

        h_pre, tail, st, hs = _mixers(PREFIX, h_pre, p, zero_tail, zero_ssd, zero_s5)
        if not last:
            h_pre = _mlp(h_pre, *mlp_w, final_w, False)
        h, _, _, _ = _mixers(MAIN, h, p, tail[0, -1], st[0], hs[0:1])
        h = _mlp(h, *mlp_w, final_w, last)

    return h.reshape(BATCH, SEQ, D_MODEL)
```

```python
import functools
import math
from typing import NamedTuple

import jax
import jax.numpy as jnp
from jax import lax
from jax.experimental import pallas as pl
from jax.experimental.pallas import tpu as pltpu

F32 = jnp.float32
BF16 = jnp.bfloat16

D_MODEL = 1024
BATCH = 32
SEQ = 2048
DEPTH = 2
N_META = 16
SSD_HEADS = 16
SSD_HEAD_DIM = 64
SSD_INNER = SSD_HEADS * SSD_HEAD_DIM
SSD_GROUPS = 4
SSD_HEADS_PER_GROUP = SSD_HEADS // SSD_GROUPS
SSD_GROUP_W = SSD_INNER // SSD_GROUPS
SSD_STATE = 128
SSD_CONV = 4
S5_WIDTH = D_MODEL // 2
S5_GROUP = 16
S5_GROUPS = S5_WIDTH // S5_GROUP
S5_STATE = 64
S5_LANES = S5_GROUPS * S5_STATE
D_FF = 4 * D_MODEL
EPS = 1e-6
XBC_WIDTH = SSD_INNER + 2 * SSD_GROUPS * SSD_STATE

LANE = 128
SUBLANE = 8

SSD_T = 128
S5_SUB = 4
S5_TC = S5_SUB * SUBLANE
S5_CTILE = 256
S5_USLABS = S5_WIDTH // LANE

OTH_W = 2 * D_MODEL + SSD_INNER + S5_WIDTH
NJ_PROJ = 4
TN_ACT = XBC_WIDTH // NJ_PROJ
TN_OTH = OTH_W // NJ_PROJ
TM_PROJ = 2048
PROJ_CHUNK = 128
CONV_ROWS = 64
TM_TOK = 1024
TM_MERGE = 512
TF_MLP = 2048
VMEM_LIMIT = 56 * 1024 * 1024


class Dims(NamedTuple):
    nb: int
    lp: int
    pad: int


MAIN = Dims(BATCH, SEQ, 0)
PREFIX = Dims(1, SSD_T, SSD_T - N_META)


def _sigmoid(x):
    return 0.5 * (jnp.tanh(0.5 * x) + 1.0)


def _rms_scale(x):
    return x * lax.rsqrt(jnp.mean(x * x, axis=-1, keepdims=True) + EPS)


def _bdot(a, b):
    return jnp.dot(a, b, preferred_element_type=F32)


def _params(n_axes):
    return pltpu.CompilerParams(
        dimension_semantics=("arbitrary",) * n_axes, vmem_limit_bytes=VMEM_LIMIT)


def _inproj_kernel(tm, h_ref, nw_ref, wx_ref, wo_ref, wdt_ref, cw_ref, cb_ref, cin_ref,
                   act_ref, oth_ref, dt_ref, tail_ref, xn_ref, ext_ref, carry_ref):
    i = pl.program_id(1)
    j = pl.program_id(2)

    @pl.when(j == 0)
    def _():
        xn = (_rms_scale(h_ref[...]) * nw_ref[...]).astype(BF16)
        xn_ref[...] = xn
        dt_ref[...] = _bdot(xn, wdt_ref[...])

    def project(lo, hi):
        xn = xn_ref[lo:hi, :]
        ext_ref[SUBLANE + lo:SUBLANE + hi, :] = _bdot(xn, wx_ref[...])
        oth_ref[lo:hi, :] = _bdot(xn, wo_ref[...]).astype(BF16)

    def conv_silu(lo, hi):
        for r0 in range(lo, hi, CONV_ROWS):
            for lb in range(TN_ACT // LANE):
                ls = slice(lb * LANE, (lb + 1) * LANE)
                xe = ext_ref[r0:r0 + SUBLANE + CONV_ROWS, ls]
                acc = cb_ref[:, ls] + cw_ref[SSD_CONV - 1:SSD_CONV, ls] * xe[SUBLANE:, :]
                for k in range(SSD_CONV - 1):
                    shifted = pltpu.roll(xe, SSD_CONV - 1 - k, 0)[SUBLANE:, :]
                    acc = acc + cw_ref[k:k + 1, ls] * shifted
                half = 0.5 * acc
                act_ref[r0:r0 + CONV_ROWS, ls] = (half * (jnp.tanh(half) + 1.0)).astype(BF16)

    ext_ref[0:SUBLANE, :] = jnp.where(i == 0, cin_ref[...], carry_ref[j])
    bounds = tuple(range(0, tm, PROJ_CHUNK)) + (tm,)
    project(bounds[0], bounds[1])
    for q in range(1, len(bounds) - 1):
        conv_silu(bounds[q - 1], bounds[q])
        project(bounds[q], bounds[q + 1])
    conv_silu(bounds[-2], bounds[-1])
    tail = ext_ref[tm:tm + SUBLANE, :]
    carry_ref[j] = tail
    tail_ref[...] = tail


def _inproj(dims, h3, norm_w, w_x, w_o, w_dt, conv_w, conv_b, conv_tail):
    tm = min(TM_PROJ, dims.lp)
    c2 = lambda b, i, j: (0, 0)
    sds = jax.ShapeDtypeStruct
    return pl.pallas_call(
        functools.partial(_inproj_kernel, tm),
        grid=(dims.nb, dims.lp // tm, NJ_PROJ),
        in_specs=[
            pl.BlockSpec((None, tm, D_MODEL), lambda b, i, j: (b, i, 0)),
            pl.BlockSpec((1, D_MODEL), c2),
            pl.BlockSpec((D_MODEL, TN_ACT), lambda b, i, j: (0, j)),
            pl.BlockSpec((D_MODEL, TN_OTH), lambda b, i, j: (0, j)),
            pl.BlockSpec((D_MODEL, LANE), c2),
            pl.BlockSpec((SSD_CONV, TN_ACT), lambda b, i, j: (0, j)),
            pl.BlockSpec((1, TN_ACT), lambda b, i, j: (0, j)),
            pl.BlockSpec((SUBLANE, TN_ACT), lambda b, i, j: (0, j)),
        ],
        out_specs=[
            pl.BlockSpec((None, tm, TN_ACT), lambda b, i, j: (b, i, j)),
            pl.BlockSpec((None, tm, TN_OTH), lambda b, i, j: (b, i, j)),
            pl.BlockSpec((None, tm, LANE), lambda b, i, j: (b, i, 0)),
            pl.BlockSpec((None, None, SUBLANE, TN_ACT), lambda b, i, j: (b, i, 0, j)),
        ],
        out_shape=[
            sds((dims.nb, dims.lp, XBC_WIDTH), BF16),
            sds((dims.nb, dims.lp, OTH_W), BF16),
            sds((dims.nb, dims.lp, LANE), F32),
            sds((dims.nb, dims.lp // tm, SUBLANE, XBC_WIDTH), F32),
        ],
        scratch_shapes=[
            pltpu.VMEM((tm, D_MODEL), BF16),
            pltpu.VMEM((tm + SUBLANE, TN_ACT), F32),
            pltpu.VMEM((NJ_PROJ, SUBLANE, TN_ACT), F32),
        ],
        compiler_params=_params(3),
        name="inproj",
    )(h3, norm_w, w_x, w_o, w_dt, conv_w, conv_b, conv_tail)


def _split_bf16(x, n, axis):
    parts = []
    for _ in range(n):
        p = x.astype(BF16)
        parts.append(p)
        x = x - p.astype(F32)
    return jnp.concatenate(parts, axis=axis)


def _ssd_kernel(pad, act_ref, z_ref, dt_ref, dtb_ref, alog_ref, exp_ref, hmask_ref, dsk_ref, nw_ref,
                stin_ref, y_ref, stout_ref, st_ref):
    c = pl.program_id(1)
    t = SSD_T

    @pl.when(c == 0)
    def _():
        st_ref[...] = stin_ref[...]

    xdt = dt_ref[...] + dtb_ref[...]
    dt = jnp.maximum(xdt, 0.0) + jnp.log1p(jnp.exp(-jnp.abs(xdt)))
    if pad:
        row = c * t + lax.broadcasted_iota(jnp.int32, (t, LANE), 0)
        dt = jnp.where(row >= pad, dt, 0.0)
    a = -jnp.exp(alog_ref[...])
    r_i = lax.broadcasted_iota(jnp.int32, (t, t), 0)
    c_i = lax.broadcasted_iota(jnp.int32, (t, t), 1)
    causal = r_i >= c_i
    tri = causal.astype(BF16)
    cs = _bdot(jnp.concatenate([tri, tri, tri], axis=1), _split_bf16(dt * a, 3, 0))
    cs_t = cs.T

    dt_ch = _bdot(_split_bf16(dt, 2, 1), exp_ref[0:2 * LANE, :])
    cs_ch = _bdot(_split_bf16(cs, 3, 1), exp_ref[...])
    cs_end = cs_ch[t - 1:t, :]
    xs = act_ref[:, :SSD_INNER].astype(F32)
    xdt = xs * dt_ch
    xdt_b = xdt.astype(BF16)
    xdt_end_b = (xdt * jnp.exp(cs_end - cs_ch)).astype(BF16)

    for g in range(SSD_GROUPS):
        gs = slice(g * SSD_GROUP_W, (g + 1) * SSD_GROUP_W)
        b_lo = SSD_INNER + g * SSD_STATE
        c_lo = SSD_INNER + SSD_GROUPS * SSD_STATE + g * SSD_STATE
        b_gt = act_ref[:, b_lo:b_lo + SSD_STATE].astype(F32).T.astype(BF16)
        c_g = act_ref[:, c_lo:c_lo + SSD_STATE]
        scores = _bdot(c_g, b_gt)
        m_parts = []
        for r in range(SSD_HEADS_PER_GROUP):
            h = g * SSD_HEADS_PER_GROUP + r
            seg = cs[:, h:h + 1] - cs_t[h:h + 1, :]
            decay = jnp.exp(jnp.where(causal, seg, -jnp.inf))
            m_parts.append((scores * decay).astype(BF16))
        x_bd = jnp.concatenate([xdt_b[:, gs]] * SSD_HEADS_PER_GROUP, axis=0) * hmask_ref[...]
        y = _bdot(jnp.concatenate(m_parts, axis=1), x_bd)
        st = st_ref[g]
        y = y + _bdot(c_g, st.astype(BF16)) * jnp.exp(cs_ch[:, gs])
        st_ref[g] = st * jnp.exp(cs_end[:, gs]) + _bdot(b_gt, xdt_end_b[:, gs])

        z = z_ref[:, gs].astype(F32)
        y = (y + dsk_ref[:, gs] * xs[:, gs]) * (z * _sigmoid(z))
        y_ref[:, gs] = (_rms_scale(y) * nw_ref[:, gs]).astype(BF16)

    @pl.when(c == pl.num_programs(1) - 1)
    def _():
        stout_ref[...] = st_ref[...]


def _ssd(dims, act3, oth3, dt3, dt_bias, a_log, d_full, norm_w, state_in):
    z_blk = 2 * D_MODEL // SSD_INNER
    c2 = lambda b, c: (0, 0)
    st_shape = (SSD_GROUPS, SSD_STATE, SSD_GROUP_W)
    head_of_ch = jnp.arange(SSD_INNER) // SSD_HEAD_DIM
    expand = (jnp.arange(LANE)[:, None] == head_of_ch[None, :]).astype(BF16)
    expand3 = jnp.tile(expand, (3, 1))
    head_of_row = jnp.arange(SSD_HEADS_PER_GROUP * SSD_T) // SSD_T
    hmask = (head_of_row[:, None] == head_of_ch[None, :SSD_GROUP_W]).astype(BF16)
    return pl.pallas_call(
        functools.partial(_ssd_kernel, dims.pad),
        grid=(dims.nb, dims.lp // SSD_T),
        in_specs=[
            pl.BlockSpec((None, SSD_T, XBC_WIDTH), lambda b, c: (b, c, 0)),
            pl.BlockSpec((None, SSD_T, SSD_INNER), lambda b, c: (b, c, z_blk)),
            pl.BlockSpec((None, SSD_T, LANE), lambda b, c: (b, c, 0)),
            pl.BlockSpec((1, LANE), c2),
            pl.BlockSpec((1, LANE), c2),
            pl.BlockSpec((3 * LANE, SSD_INNER), c2),
            pl.BlockSpec((SSD_HEADS_PER_GROUP * SSD_T, SSD_GROUP_W), c2),
            pl.BlockSpec((1, SSD_INNER), c2),
            pl.BlockSpec((1, SSD_INNER), c2),
            pl.BlockSpec(st_shape, lambda b, c: (0, 0, 0)),
        ],
        out_specs=[
            pl.BlockSpec((None, SSD_T, SSD_INNER), lambda b, c: (b, c, 0)),
            pl.BlockSpec((None,) + st_shape, lambda b, c: (b, 0, 0, 0)),
        ],
        out_shape=[
            jax.ShapeDtypeStruct((dims.nb, dims.lp, SSD_INNER), BF16),
            jax.ShapeDtypeStruct((dims.nb,) + st_shape, F32),
        ],
        scratch_shapes=[pltpu.VMEM(st_shape, F32)],
        compiler_params=_params(2),
        name="ssd",
    )(act3, oth3, dt3, dt_bias, a_log, expand3, hmask, d_full, norm_w, state_in)


def _s5_disc_kernel(are_ref, aim_ref, ls_ref, bre_ref, bim_ref, abre_ref, abim_ref, bbre_ref, bbim_ref):
    step = jnp.exp(ls_ref[...])
    lam_re = are_ref[...]
    lam_im = aim_ref[...]
    mag = jnp.exp(lam_re * step)
    ab_re = mag * jnp.cos(lam_im * step)
    ab_im = mag * jnp.sin(lam_im * step)
    den = lam_re * lam_re + lam_im * lam_im
    nr = ab_re - 1.0
    f_re = (nr * lam_re + ab_im * lam_im) / den
    f_im = (ab_im * lam_re - nr * lam_im) / den
    br = bre_ref[...]
    bi = bim_ref[...]
    abre_ref[...] = ab_re
    abim_ref[...] = ab_im
    bbre_ref[...] = f_re * br - f_im * bi
    bbim_ref[...] = f_re * bi + f_im * br


def _s5_disc(a_re, a_im, log_step, b_re_t, b_im_t):
    sds = jax.ShapeDtypeStruct
    return pl.pallas_call(
        _s5_disc_kernel,
        out_shape=[sds(a_re.shape, F32), sds(a_re.shape, F32), sds(b_re_t.shape, F32), sds(b_re_t.shape, F32)],
        name="s5_disc",
    )(a_re, a_im, log_step, b_re_t, b_im_t)


def _s5_kernel(nb, u_ref, bre_ref, bim_ref, are_ref, aim_ref, cre_ref, cim_ref, hin_ref,
               y_ref, hout_ref, uslab_ref, yslab_ref, bu_ref, h_ref):
    @pl.when(pl.program_id(0) == 0)
    def _():
        h_ref[...] = jnp.broadcast_to(hin_ref[...], h_ref.shape)

    blk_rows = nb * SUBLANE

    u32 = u_ref[...].astype(F32)
    for q in range(S5_SUB):
        uq = u32[:, q * SUBLANE:(q + 1) * SUBLANE, :].reshape(blk_rows, S5_WIDTH)
        for k in range(S5_USLABS):
            uslab_ref[q * S5_USLABS + k] = uq[:, k * LANE:(k + 1) * LANE]
    steps = []
    for t in range(S5_TC):
        q, tl = divmod(t, SUBLANE)
        rows = pl.ds(tl, nb, stride=SUBLANE)
        steps.append(jnp.concatenate(
            [uslab_ref[q * S5_USLABS + k, rows, :] for k in range(S5_USLABS)], axis=1))
    u = jnp.concatenate(steps, axis=0).astype(BF16)

    n_half = bre_ref.shape[0]
    kw = S5_WIDTH // n_half
    nw = S5_LANES // n_half
    pieces = []
    for j in range(S5_LANES // S5_CTILE):
        k, off = divmod(j * S5_CTILE, nw)
        ks = slice(j * S5_CTILE, (j + 1) * S5_CTILE)
        ki = slice(S5_LANES + j * S5_CTILE, S5_LANES + (j + 1) * S5_CTILE)
        uk = u[:, k * kw:(k + 1) * kw]
        bu_ref[:, ks] = _bdot(uk, bre_ref[k, :, off:off + S5_CTILE])
        bu_ref[:, ki] = _bdot(uk, bim_ref[k, :, off:off + S5_CTILE])
        for s in range(j * S5_CTILE // LANE, (j + 1) * S5_CTILE // LANE):
            ls = slice(s * LANE, (s + 1) * LANE)
            li = slice(S5_LANES + s * LANE, S5_LANES + (s + 1) * LANE)
            a_r = are_ref[:, ls]
            a_i = aim_ref[:, ls]
            h_r = h_ref[:, ls]
            h_i = h_ref[:, li]
            for t in range(S5_TC):
                rows = slice(t * nb, (t + 1) * nb)
                n_r = a_r * h_r - a_i * h_i + bu_ref[rows, ls]
                n_i = a_r * h_i + a_i * h_r + bu_ref[rows, li]
                h_r, h_i = n_r, n_i
                bu_ref[rows, ls] = h_r
                bu_ref[rows, li] = h_i
            h_ref[:, ls] = h_r
            h_ref[:, li] = h_i
        pieces.append(_bdot(bu_ref[:, ks].astype(BF16), cre_ref[j])
                      - _bdot(bu_ref[:, ki].astype(BF16), cim_ref[j]))
    y = jnp.concatenate(pieces, axis=1)

    for t in range(S5_TC):
        q, tl = divmod(t, SUBLANE)
        rows = pl.ds(tl, nb, stride=SUBLANE)
        for k in range(S5_USLABS):
            yslab_ref[q * S5_USLABS + k, rows, :] = y[t * nb:(t + 1) * nb, k * LANE:(k + 1) * LANE]
    halves = []
    for q in range(S5_SUB):
        yq = jnp.concatenate([yslab_ref[q * S5_USLABS + k] for k in range(S5_USLABS)], axis=1)
        halves.append(yq.reshape(nb, SUBLANE, S5_WIDTH))
    y_ref[...] = jnp.concatenate(halves, axis=1).astype(BF16)

    @pl.when(pl.program_id(0) == pl.num_programs(0) - 1)
    def _():
        hout_ref[...] = h_ref[...]


def _s5(dims, oth3, bre_bd, bim_bd, ab_re, ab_im, cre_bd, cim_bd, state_in):
    nb = dims.nb
    u_blk = (OTH_W - S5_WIDTH) // S5_WIDTH
    c2 = lambda i: (0, 0)
    c3 = lambda i: (0, 0, 0)
    return pl.pallas_call(
        functools.partial(_s5_kernel, nb),
        grid=(dims.lp // S5_TC,),
        in_specs=[
            pl.BlockSpec((nb, S5_TC, S5_WIDTH), lambda i: (0, i, u_blk)),
            pl.BlockSpec(bre_bd.shape, c3),
            pl.BlockSpec(bim_bd.shape, c3),
            pl.BlockSpec((1, S5_LANES), c2),
            pl.BlockSpec((1, S5_LANES), c2),
            pl.BlockSpec(cre_bd.shape, c3),
            pl.BlockSpec(cim_bd.shape, c3),
            pl.BlockSpec((1, 2 * S5_LANES), c2),
        ],
        out_specs=[
            pl.BlockSpec((nb, S5_TC, S5_WIDTH), lambda i: (0, i, 0)),
            pl.BlockSpec((nb, 2 * S5_LANES), c2),
        ],
        out_shape=[
            jax.ShapeDtypeStruct((nb, dims.lp, S5_WIDTH), BF16),
            jax.ShapeDtypeStruct((nb, 2 * S5_LANES), F32),
        ],
        scratch_shapes=[
            pltpu.VMEM((S5_SUB * S5_USLABS, nb * SUBLANE, LANE), F32),
            pltpu.VMEM((S5_SUB * S5_USLABS, nb * SUBLANE, LANE), F32),
            pltpu.VMEM((nb * S5_TC, 2 * S5_LANES), F32),
            pltpu.VMEM((nb, 2 * S5_LANES), F32),
        ],
        compiler_params=_params(1),
        name="s5",
    )(oth3, bre_bd, bim_bd, ab_re, ab_im, cre_bd, cim_bd, state_in)


def _merge_kernel(g_ref, ya_ref, ys_ref, u_ref, h_ref, sd_ref, wglu_ref, wout_ref, o_ref):
    ys = ys_ref[...].astype(F32) + sd_ref[...] * u_ref[...].astype(F32)
    cdf = 0.5 * (1.0 + jnp.tanh(math.sqrt(2.0 / math.pi) * (ys + 0.044715 * (ys * ys * ys))))
    vg = _bdot((ys * cdf).astype(BF16), wglu_ref[...])
    yb = vg[:, :D_MODEL] * _sigmoid(vg[:, D_MODEL:])
    g = g_ref[...].astype(F32)
    m = _sigmoid(g[:, :D_MODEL]) * ya_ref[...].astype(F32) + _sigmoid(g[:, D_MODEL:]) * yb
    o_ref[...] = h_ref[...] + _bdot(m.astype(BF16), wout_ref[...])


def _merge(oth, y_a, y_s5, h, s5_d, w_glu, w_out):
    rows = h.shape[0]
    tm = min(TM_MERGE, rows)
    u_blk = (OTH_W - S5_WIDTH) // S5_WIDTH
    const = lambda i: (0, 0)
    return pl.pallas_call(
        _merge_kernel,
        grid=(rows // tm,),
        in_specs=[
            pl.BlockSpec((tm, 2 * D_MODEL), lambda i: (i, 0)),
            pl.BlockSpec((tm, SSD_INNER), lambda i: (i, 0)),
            pl.BlockSpec((tm, S5_WIDTH), lambda i: (i, 0)),
            pl.BlockSpec((tm, S5_WIDTH), lambda i: (i, u_blk)),
            pl.BlockSpec((tm, D_MODEL), lambda i: (i, 0)),
            pl.BlockSpec((1, S5_WIDTH), const),
            pl.BlockSpec((S5_WIDTH, 2 * D_MODEL), const),
            pl.BlockSpec((D_MODEL, D_MODEL), const),
        ],
        out_specs=pl.BlockSpec((tm, D_MODEL), lambda i: (i, 0)),
        out_shape=jax.ShapeDtypeStruct((rows, D_MODEL), F32),
        compiler_params=_params(1),
        name="merge",
    )(oth, y_a, y_s5, oth, h, s5_d, w_glu, w_out)


def _mlp_kernel(final, h_ref, nw_ref, w1_ref, w2_ref, fw_ref, o_ref, xn_ref):
    f = pl.program_id(1)

    @pl.when(f == 0)
    def _():
        x = h_ref[...]
        xn_ref[...] = (_rms_scale(x) * nw_ref[...]).astype(BF16)
        o_ref[...] = x

    hid = jnp.square(jnp.maximum(_bdot(xn_ref[...], w1_ref[...]), 0.0))
    o_ref[...] += _bdot(hid.astype(BF16), w2_ref[...])

    if final:
        @pl.when(f == pl.num_programs(1) - 1)
        def _():
            o_ref[...] = _rms_scale(o_ref[...]) * fw_ref[...]


def _mlp(h, norm_w, w1, w2, final_w, final):
    rows = h.shape[0]
    tm = min(TM_TOK, rows)
    return pl.pallas_call(
        functools.partial(_mlp_kernel, final),
        grid=(rows // tm, D_FF // TF_MLP),
        in_specs=[
            pl.BlockSpec((tm, D_MODEL), lambda i, f: (i, 0)),
            pl.BlockSpec((1, D_MODEL), lambda i, f: (0, 0)),
            pl.BlockSpec((D_MODEL, TF_MLP), lambda i, f: (0, f)),
            pl.BlockSpec((TF_MLP, D_MODEL), lambda i, f: (f, 0)),
            pl.BlockSpec((1, D_MODEL), lambda i, f: (0, 0)),
        ],
        out_specs=pl.BlockSpec((tm, D_MODEL), lambda i, f: (i, 0)),
        out_shape=jax.ShapeDtypeStruct((rows, D_MODEL), F32),
        scratch_shapes=[pltpu.VMEM((tm, D_MODEL), BF16)],
        compiler_params=_params(2),
        name="mlp",
    )(h, norm_w, w1, w2, final_w)


def _pad_lanes(v):
    return jnp.pad(v, (0, LANE - v.shape[0])).reshape(1, LANE)


def _block_diag(blocks):
    s, n, r, c = blocks.shape
    eye = jnp.eye(n, dtype=blocks.dtype)
    return (blocks[:, :, :, None, :] * eye[None, :, None, :, None]).reshape(s, n * r, n * c)


def _mixers(dims, h, p, conv_tail, ssd_state, s5_state):
    act3, oth3, dt3, tail = _inproj(
        dims, h.reshape(dims.nb, dims.lp, D_MODEL), p["norm_mix_w"], p["w_x"], p["w_o"], p["w_dt"],
        p["conv_w"], p["conv_b"], conv_tail)
    y_a, st = _ssd(dims, act3, oth3, dt3, p["dt_bias"], p["a_log"], p["d_full"], p["ssd_norm_w"],
                   ssd_state)
    s5_dims = dims._replace(nb=max(dims.nb, SUBLANE))
    y_s5, hs = _s5(s5_dims, jnp.broadcast_to(oth3, (s5_dims.nb,) + oth3.shape[1:]),
                   p["bre_bd"], p["bim_bd"], p["ab_re"], p["ab_im"], p["cre_bd"], p["cim_bd"], s5_state)
    y_s5 = y_s5[:dims.nb]
    rows = dims.nb * dims.lp
    h = _merge(oth3.reshape(rows, OTH_W), y_a.reshape(rows, SSD_INNER), y_s5.reshape(rows, S5_WIDTH),
               h, p["s5_d"], p["w_glu"], p["w_out"])
    return h, tail, st, hs


def kernel(x, meta_tokens, w_in, conv_w, conv_b, dt_bias, ssd_a_log, ssd_d, ssd_norm_w, s5_a_re, s5_a_im, s5_log_step, s5_b_re, s5_b_im, s5_c_re, s5_c_im, s5_d, w_glu, w_out, norm_mix_w, norm_mlp_w, w_ff_in, w_ff_out, final_norm_w):
    o_xbc = SSD_INNER
    o_dt = o_xbc + XBC_WIDTH
    o_u = o_dt + SSD_HEADS
    o_g = o_u + S5_WIDTH
    n_half = 2
    gph = S5_GROUPS // n_half
    gpc = S5_CTILE // S5_STATE
    final_w = final_norm_w.reshape(1, D_MODEL)

    h = x.reshape(BATCH * SEQ, D_MODEL)
    meta_seq = jnp.concatenate(
        [jnp.zeros((PREFIX.pad, D_MODEL), x.dtype), meta_tokens.astype(x.dtype)], axis=0)
    h_pre = jnp.broadcast_to(meta_seq[None], (PREFIX.nb, PREFIX.lp, D_MODEL)).reshape(-1, D_MODEL)
    zero_tail = jnp.zeros((SUBLANE, XBC_WIDTH), F32)
    zero_ssd = jnp.zeros((SSD_GROUPS, SSD_STATE, SSD_GROUP_W), F32)
    zero_s5 = jnp.zeros((1, 2 * S5_LANES), F32)

    for i in range(DEPTH):
        wi = w_in[i]
        ab_re, ab_im, bb_re_t, bb_im_t = _s5_disc(
            s5_a_re[i].reshape(S5_GROUPS, 1, S5_STATE), s5_a_im[i].reshape(S5_GROUPS, 1, S5_STATE),
            s5_log_step[i].reshape(S5_GROUPS, 1, 1),
            jnp.swapaxes(s5_b_re[i], 1, 2), jnp.swapaxes(s5_b_im[i], 1, 2))
        bd_b = lambda bt: _block_diag(
            bt.reshape(n_half, gph, S5_GROUP, S5_STATE)).astype(BF16)
        bd_c = lambda cm: _block_diag(
            jnp.swapaxes(cm, 1, 2).reshape(S5_GROUPS // gpc, gpc, S5_STATE, S5_GROUP)).astype(BF16)
        p = {
            "norm_mix_w": norm_mix_w[i].reshape(1, D_MODEL),
            "w_x": wi[:, o_xbc:o_dt].astype(BF16),
            "w_o": jnp.concatenate([wi[:, o_g:], wi[:, :o_xbc], wi[:, o_u:o_g]], axis=1).astype(BF16),
            "w_dt": jnp.pad(wi[:, o_dt:o_u], ((0, 0), (0, LANE - SSD_HEADS))).astype(BF16),
            "conv_w": conv_w[i],
            "conv_b": conv_b[i].reshape(1, XBC_WIDTH),
            "dt_bias": _pad_lanes(dt_bias[i]),
            "a_log": _pad_lanes(ssd_a_log[i]),
            "d_full": jnp.repeat(ssd_d[i], SSD_HEAD_DIM).reshape(1, SSD_INNER),
            "ssd_norm_w": ssd_norm_w[i].reshape(1, SSD_INNER),
            "bre_bd": bd_b(bb_re_t),
            "bim_bd": bd_b(bb_im_t),
            "ab_re": ab_re.reshape(1, S5_LANES),
            "ab_im": ab_im.reshape(1, S5_LANES),
            "cre_bd": bd_c(s5_c_re[i]),
            "cim_bd": bd_c(s5_c_im[i]),
            "s5_d": s5_d[i].reshape(1, S5_WIDTH),
            "w_glu": w_glu[i].astype(BF16),
            "w_out": w_out[i].astype(BF16),
        }
        mlp_w = (norm_mlp_w[i].reshape(1, D_MODEL), w_ff_in[i].astype(BF16), w_ff_out[i].astype(BF16))
        last = i == DEPTH - 1

        h_pre, tail, st, hs = _mixers(PREFIX, h_pre, p, zero_tail, zero_ssd, zero_s5)
        if not last:
            h_pre = _mlp(h_pre, *mlp_w, final_w, False)
        h, _, _, _ = _mixers(MAIN, h, p, tail[0, -1], st[0], hs[0:1])
        h = _mlp(h, *mlp_w, final_w, last)

    return h.reshape(BATCH, SEQ, D_MODEL)
```

```python
import functools
import math
from typing import NamedTuple

import jax
import jax.numpy as jnp
from jax import lax
from jax.experimental import pallas as pl
from jax.experimental.pallas import tpu as pltpu

F32 = jnp.float32
BF16 = jnp.bfloat16

D_MODEL = 1024
BATCH = 32
SEQ = 2048
DEPTH = 2
N_META = 16
SSD_HEADS = 16
SSD_HEAD_DIM = 64
SSD_INNER = SSD_HEADS * SSD_HEAD_DIM
SSD_GROUPS = 4
SSD_HEADS_PER_GROUP = SSD_HEADS // SSD_GROUPS
SSD_GROUP_W = SSD_INNER // SSD_GROUPS
SSD_STATE = 128
SSD_CONV = 4
S5_WIDTH = D_MODEL // 2
S5_GROUP = 16
S5_GROUPS = S5_WIDTH // S5_GROUP
S5_STATE = 64
S5_LANES = S5_GROUPS * S5_STATE
D_FF = 4 * D_MODEL
EPS = 1e-6
XBC_WIDTH = SSD_INNER + 2 * SSD_GROUPS * SSD_STATE

LANE = 128
SUBLANE = 8

SSD_T = 128
S5_SUB = 2
S5_TC = S5_SUB * SUBLANE
S5_CTILE = 256
S5_USLABS = S5_WIDTH // LANE

OTH_W = 2 * D_MODEL + SSD_INNER + S5_WIDTH
NJ_PROJ = 4
TN_ACT = XBC_WIDTH // NJ_PROJ
TN_OTH = OTH_W // NJ_PROJ
TM_PROJ = 2048
PROJ_CHUNK = 256
CONV_ROWS = 64
TM_MERGE = 512
TF_MLP = 2048
VMEM_LIMIT = 56 * 1024 * 1024


class Dims(NamedTuple):
    nb: int
    lp: int
    pad: int


MAIN = Dims(BATCH, SEQ, 0)
PREFIX = Dims(1, SSD_T, SSD_T - N_META)


def _sigmoid(x):
    return 0.5 * (jnp.tanh(0.5 * x) + 1.0)


def _rms_scale(x):
    return x * lax.rsqrt(jnp.mean(x * x, axis=-1, keepdims=True) + EPS)


def _bdot(a, b):
    return jnp.dot(a, b, preferred_element_type=F32)


def _params(n_axes):
    return pltpu.CompilerParams(
        dimension_semantics=("arbitrary",) * n_axes, vmem_limit_bytes=VMEM_LIMIT)


def _inproj_kernel(tm, h_ref, nw_ref, wx_ref, wo_ref, wdt_ref, cw_ref, cb_ref, cin_ref,
                   act_ref, oth_ref, dt_ref, tail_ref, xn_ref, ext_ref, carry_ref):
    i = pl.program_id(1)
    j = pl.program_id(2)

    @pl.when(j == 0)
    def _():
        xn = (_rms_scale(h_ref[...]) * nw_ref[...]).astype(BF16)
        xn_ref[...] = xn
        dt_ref[...] = _bdot(xn, wdt_ref[...])

    def project(lo, hi):
        xn = xn_ref[lo:hi, :]
        ext_ref[SUBLANE + lo:SUBLANE + hi, :] = _bdot(xn, wx_ref[...])
        oth_ref[lo:hi, :] = _bdot(xn, wo_ref[...]).astype(BF16)

    def conv_silu(lo, hi):
        for r0 in range(lo, hi, CONV_ROWS):
            for lb in range(TN_ACT // LANE):
                ls = slice(lb * LANE, (lb + 1) * LANE)
                xe = ext_ref[r0:r0 + SUBLANE + CONV_ROWS, ls]
                acc = cb_ref[:, ls] + cw_ref[SSD_CONV - 1:SSD_CONV, ls] * xe[SUBLANE:, :]
                for k in range(SSD_CONV - 1):
                    shifted = pltpu.roll(xe, SSD_CONV - 1 - k, 0)[SUBLANE:, :]
                    acc = acc + cw_ref[k:k + 1, ls] * shifted
                half = 0.5 * acc
                act_ref[r0:r0 + CONV_ROWS, ls] = (half * (jnp.tanh(half) + 1.0)).astype(BF16)

    ext_ref[0:SUBLANE, :] = jnp.where(i == 0, cin_ref[...], carry_ref[j])
    bounds = tuple(range(0, tm, PROJ_CHUNK)) + (tm,)
    project(bounds[0], bounds[1])
    for q in range(1, len(bounds) - 1):
        conv_silu(bounds[q - 1], bounds[q])
        project(bounds[q], bounds[q + 1])
    conv_silu(bounds[-2], bounds[-1])
    tail = ext_ref[tm:tm + SUBLANE, :]
    carry_ref[j] = tail
    tail_ref[...] = tail


def _inproj(dims, h3, norm_w, w_x, w_o, w_dt, conv_w, conv_b, conv_tail):
    tm = min(TM_PROJ, dims.lp)
    c2 = lambda b, i, j: (0, 0)
    sds = jax.ShapeDtypeStruct
    return pl.pallas_call(
        functools.partial(_inproj_kernel, tm),
        grid=(dims.nb, dims.lp // tm, NJ_PROJ),
        in_specs=[
            pl.BlockSpec((None, tm, D_MODEL), lambda b, i, j: (b, i, 0)),
            pl.BlockSpec((1, D_MODEL), c2),
            pl.BlockSpec((D_MODEL, TN_ACT), lambda b, i, j: (0, j)),
            pl.BlockSpec((D_MODEL, TN_OTH), lambda b, i, j: (0, j)),
            pl.BlockSpec((D_MODEL, LANE), c2),
            pl.BlockSpec((SSD_CONV, TN_ACT), lambda b, i, j: (0, j)),
            pl.BlockSpec((1, TN_ACT), lambda b, i, j: (0, j)),
            pl.BlockSpec((SUBLANE, TN_ACT), lambda b, i, j: (0, j)),
        ],
        out_specs=[
            pl.BlockSpec((None, tm, TN_ACT), lambda b, i, j: (b, i, j)),
            pl.BlockSpec((None, tm, TN_OTH), lambda b, i, j: (b, i, j)),
            pl.BlockSpec((None, tm, LANE), lambda b, i, j: (b, i, 0)),
            pl.BlockSpec((None, None, SUBLANE, TN_ACT), lambda b, i, j: (b, i, 0, j)),
        ],
        out_shape=[
            sds((dims.nb, dims.lp, XBC_WIDTH), BF16),
            sds((dims.nb, dims.lp, OTH_W), BF16),
            sds((dims.nb, dims.lp, LANE), F32),
            sds((dims.nb, dims.lp // tm, SUBLANE, XBC_WIDTH), F32),
        ],
        scratch_shapes=[
            pltpu.VMEM((tm, D_MODEL), BF16),
            pltpu.VMEM((tm + SUBLANE, TN_ACT), F32),
            pltpu.VMEM((NJ_PROJ, SUBLANE, TN_ACT), F32),
        ],
        compiler_params=_params(3),
        name="inproj",
    )(h3, norm_w, w_x, w_o, w_dt, conv_w, conv_b, conv_tail)


def _split_bf16(x, n, axis):
    parts = []
    for _ in range(n):
        p = x.astype(BF16)
        parts.append(p)
        x = x - p.astype(F32)
    return jnp.concatenate(parts, axis=axis)


def _ssd_kernel(pad, act_ref, z_ref, dt_ref, dtb_ref, alog_ref, exp_ref, hmask_ref, dsk_ref, nw_ref,
                stin_ref, y_ref, stout_ref, st_ref):
    c = pl.program_id(1)
    t = SSD_T

    @pl.when(c == 0)
    def _():
        st_ref[...] = stin_ref[...]

    xdt = dt_ref[...] + dtb_ref[...]
    dt = jnp.maximum(xdt, 0.0) + jnp.log1p(jnp.exp(-jnp.abs(xdt)))
    if pad:
        row = c * t + lax.broadcasted_iota(jnp.int32, (t, LANE), 0)
        dt = jnp.where(row >= pad, dt, 0.0)
    a = -jnp.exp(alog_ref[...])
    r_i = lax.broadcasted_iota(jnp.int32, (t, t), 0)
    c_i = lax.broadcasted_iota(jnp.int32, (t, t), 1)
    causal = r_i >= c_i
    tri = causal.astype(BF16)
    cs = _bdot(jnp.concatenate([tri, tri, tri], axis=1), _split_bf16(dt * a, 3, 0))
    cs_t = cs.T

    dt_ch = _bdot(_split_bf16(dt, 2, 1), exp_ref[0:2 * LANE, :])
    cs_ch = _bdot(_split_bf16(cs, 3, 1), exp_ref[...])
    cs_end = cs_ch[t - 1:t, :]
    xs = act_ref[:, :SSD_INNER].astype(F32)
    xdt = xs * dt_ch
    xdt_b = xdt.astype(BF16)
    xdt_end_b = (xdt * jnp.exp(cs_end - cs_ch)).astype(BF16)

    for g in range(SSD_GROUPS):
        gs = slice(g * SSD_GROUP_W, (g + 1) * SSD_GROUP_W)
        b_lo = SSD_INNER + g * SSD_STATE
        c_lo = SSD_INNER + SSD_GROUPS * SSD_STATE + g * SSD_STATE
        b_gt = act_ref[:, b_lo:b_lo + SSD_STATE].astype(F32).T.astype(BF16)
        c_g = act_ref[:, c_lo:c_lo + SSD_STATE]
        scores = _bdot(c_g, b_gt)
        m_parts = []
        for r in range(SSD_HEADS_PER_GROUP):
            h = g * SSD_HEADS_PER_GROUP + r
            seg = cs[:, h:h + 1] - cs_t[h:h + 1, :]
            decay = jnp.exp(jnp.where(causal, seg, -jnp.inf))
            m_parts.append((scores * decay).astype(BF16))
        x_bd = jnp.concatenate([xdt_b[:, gs]] * SSD_HEADS_PER_GROUP, axis=0) * hmask_ref[...]
        y = _bdot(jnp.concatenate(m_parts, axis=1), x_bd)
        st = st_ref[g]
        y = y + _bdot(c_g, st.astype(BF16)) * jnp.exp(cs_ch[:, gs])
        st_ref[g] = st * jnp.exp(cs_end[:, gs]) + _bdot(b_gt, xdt_end_b[:, gs])

        z = z_ref[:, gs].astype(F32)
        y = (y + dsk_ref[:, gs] * xs[:, gs]) * (z * _sigmoid(z))
        y_ref[:, gs] = (_rms_scale(y) * nw_ref[:, gs]).astype(BF16)

    @pl.when(c == pl.num_programs(1) - 1)
    def _():
        stout_ref[...] = st_ref[...]


def _ssd(dims, act3, oth3, dt3, dt_bias, a_log, d_full, norm_w, state_in):
    z_blk = 2 * D_MODEL // SSD_INNER
    c2 = lambda b, c: (0, 0)
    st_shape = (SSD_GROUPS, SSD_STATE, SSD_GROUP_W)
    head_of_ch = jnp.arange(SSD_INNER) // SSD_HEAD_DIM
    expand = (jnp.arange(LANE)[:, None] == head_of_ch[None, :]).astype(BF16)
    expand3 = jnp.tile(expand, (3, 1))
    head_of_row = jnp.arange(SSD_HEADS_PER_GROUP * SSD_T) // SSD_T
    hmask = (head_of_row[:, None] == head_of_ch[None, :SSD_GROUP_W]).astype(BF16)
    return pl.pallas_call(
        functools.partial(_ssd_kernel, dims.pad),
        grid=(dims.nb, dims.lp // SSD_T),
        in_specs=[
            pl.BlockSpec((None, SSD_T, XBC_WIDTH), lambda b, c: (b, c, 0)),
            pl.BlockSpec((None, SSD_T, SSD_INNER), lambda b, c: (b, c, z_blk)),
            pl.BlockSpec((None, SSD_T, LANE), lambda b, c: (b, c, 0)),
            pl.BlockSpec((1, LANE), c2),
            pl.BlockSpec((1, LANE), c2),
            pl.BlockSpec((3 * LANE, SSD_INNER), c2),
            pl.BlockSpec((SSD_HEADS_PER_GROUP * SSD_T, SSD_GROUP_W), c2),
            pl.BlockSpec((1, SSD_INNER), c2),
            pl.BlockSpec((1, SSD_INNER), c2),
            pl.BlockSpec(st_shape, lambda b, c: (0, 0, 0)),
        ],
        out_specs=[
            pl.BlockSpec((None, SSD_T, SSD_INNER), lambda b, c: (b, c, 0)),
            pl.BlockSpec((None,) + st_shape, lambda b, c: (b, 0, 0, 0)),
        ],
        out_shape=[
            jax.ShapeDtypeStruct((dims.nb, dims.lp, SSD_INNER), BF16),
            jax.ShapeDtypeStruct((dims.nb,) + st_shape, F32),
        ],
        scratch_shapes=[pltpu.VMEM(st_shape, F32)],
        compiler_params=_params(2),
        name="ssd",
    )(act3, oth3, dt3, dt_bias, a_log, expand3, hmask, d_full, norm_w, state_in)


def _s5_disc_kernel(are_ref, aim_ref, ls_ref, bre_ref, bim_ref, abre_ref, abim_ref, bbre_ref, bbim_ref):
    step = jnp.exp(ls_ref[...])
    lam_re = are_ref[...]
    lam_im = aim_ref[...]
    mag = jnp.exp(lam_re * step)
    ab_re = mag * jnp.cos(lam_im * step)
    ab_im = mag * jnp.sin(lam_im * step)
    den = lam_re * lam_re + lam_im * lam_im
    nr = ab_re - 1.0
    f_re = (nr * lam_re + ab_im * lam_im) / den
    f_im = (ab_im * lam_re - nr * lam_im) / den
    br = bre_ref[...]
    bi = bim_ref[...]
    abre_ref[...] = ab_re
    abim_ref[...] = ab_im
    bbre_ref[...] = f_re * br - f_im * bi
    bbim_ref[...] = f_re * bi + f_im * br


def _s5_disc(a_re, a_im, log_step, b_re_t, b_im_t):
    sds = jax.ShapeDtypeStruct
    return pl.pallas_call(
        _s5_disc_kernel,
        out_shape=[sds(a_re.shape, F32), sds(a_re.shape, F32), sds(b_re_t.shape, F32), sds(b_re_t.shape, F32)],
        name="s5_disc",
    )(a_re, a_im, log_step, b_re_t, b_im_t)


def _s5_kernel(nb, u_ref, bre_ref, bim_ref, are_ref, aim_ref, cre_ref, cim_ref, hin_ref,
               y_ref, hout_ref, uslab_ref, yslab_ref, bu_ref, h_ref):
    @pl.when(pl.program_id(0) == 0)
    def _():
        h_ref[...] = jnp.broadcast_to(hin_ref[...], h_ref.shape)

    blk_rows = nb * SUBLANE

    u32 = u_ref[...].astype(F32)
    for q in range(S5_SUB):
        uq = u32[:, q * SUBLANE:(q + 1) * SUBLANE, :].reshape(blk_rows, S5_WIDTH)
        for k in range(S5_USLABS):
            uslab_ref[q * S5_USLABS + k] = uq[:, k * LANE:(k + 1) * LANE]
    steps = []
    for t in range(S5_TC):
        q, tl = divmod(t, SUBLANE)
        rows = pl.ds(tl, nb, stride=SUBLANE)
        steps.append(jnp.concatenate(
            [uslab_ref[q * S5_USLABS + k, rows, :] for k in range(S5_USLABS)], axis=1))
    u = jnp.concatenate(steps, axis=0).astype(BF16)

    n_half = bre_ref.shape[0]
    kw = S5_WIDTH // n_half
    nw = S5_LANES // n_half
    pieces = []
    for j in range(S5_LANES // S5_CTILE):
        k, off = divmod(j * S5_CTILE, nw)
        ks = slice(j * S5_CTILE, (j + 1) * S5_CTILE)
        ki = slice(S5_LANES + j * S5_CTILE, S5_LANES + (j + 1) * S5_CTILE)
        uk = u[:, k * kw:(k + 1) * kw]
        bu_ref[:, ks] = _bdot(uk, bre_ref[k, :, off:off + S5_CTILE])
        bu_ref[:, ki] = _bdot(uk, bim_ref[k, :, off:off + S5_CTILE])
        for s in range(j * S5_CTILE // LANE, (j + 1) * S5_CTILE // LANE):
            ls = slice(s * LANE, (s + 1) * LANE)
            li = slice(S5_LANES + s * LANE, S5_LANES + (s + 1) * LANE)
            a_r = are_ref[:, ls]
            a_i = aim_ref[:, ls]
            h_r = h_ref[:, ls]
            h_i = h_ref[:, li]
            for t in range(S5_TC):
                rows = slice(t * nb, (t + 1) * nb)
                n_r = a_r * h_r - a_i * h_i + bu_ref[rows, ls]
                n_i = a_r * h_i + a_i * h_r + bu_ref[rows, li]
                h_r, h_i = n_r, n_i
                bu_ref[rows, ls] = h_r
                bu_ref[rows, li] = h_i
            h_ref[:, ls] = h_r
            h_ref[:, li] = h_i
        pieces.append(_bdot(bu_ref[:, ks].astype(BF16), cre_ref[j])
                      - _bdot(bu_ref[:, ki].astype(BF16), cim_ref[j]))
    y = jnp.concatenate(pieces, axis=1)

    for t in range(S5_TC):
        q, tl = divmod(t, SUBLANE)
        rows = pl.ds(tl, nb, stride=SUBLANE)
        for k in range(S5_USLABS):
            yslab_ref[q * S5_USLABS + k, rows, :] = y[t * nb:(t + 1) * nb, k * LANE:(k + 1) * LANE]
    halves = []
    for q in range(S5_SUB):
        yq = jnp.concatenate([yslab_ref[q * S5_USLABS + k] for k in range(S5_USLABS)], axis=1)
        halves.append(yq.reshape(nb, SUBLANE, S5_WIDTH))
    y_ref[...] = jnp.concatenate(halves, axis=1).astype(BF16)

    @pl.when(pl.program_id(0) == pl.num_programs(0) - 1)
    def _():
        hout_ref[...] = h_ref[...]


def _s5(dims, oth3, bre_bd, bim_bd, ab_re, ab_im, cre_bd, cim_bd, state_in):
    nb = dims.nb
    u_blk = (OTH_W - S5_WIDTH) // S5_WIDTH
    c2 = lambda i: (0, 0)
    c3 = lambda i: (0, 0, 0)
    return pl.pallas_call(
        functools.partial(_s5_kernel, nb),
        grid=(dims.lp // S5_TC,),
        in_specs=[
            pl.BlockSpec((nb, S5_TC, S5_WIDTH), lambda i: (0, i, u_blk)),
            pl.BlockSpec(bre_bd.shape, c3),
            pl.BlockSpec(bim_bd.shape, c3),
            pl.BlockSpec((1, S5_LANES), c2),
            pl.BlockSpec((1, S5_LANES), c2),
            pl.BlockSpec(cre_bd.shape, c3),
            pl.BlockSpec(cim_bd.shape, c3),
            pl.BlockSpec((1, 2 * S5_LANES), c2),
        ],
        out_specs=[
            pl.BlockSpec((nb, S5_TC, S5_WIDTH), lambda i: (0, i, 0)),
            pl.BlockSpec((nb, 2 * S5_LANES), c2),
        ],
        out_shape=[
            jax.ShapeDtypeStruct((nb, dims.lp, S5_WIDTH), BF16),
            jax.ShapeDtypeStruct((nb, 2 * S5_LANES), F32),
        ],
        scratch_shapes=[
            pltpu.VMEM((S5_SUB * S5_USLABS, nb * SUBLANE, LANE), F32),
            pltpu.VMEM((S5_SUB * S5_USLABS, nb * SUBLANE, LANE), F32),
            pltpu.VMEM((nb * S5_TC, 2 * S5_LANES), F32),
            pltpu.VMEM((nb, 2 * S5_LANES), F32),
        ],
        compiler_params=_params(1),
        name="s5",
    )(oth3, bre_bd, bim_bd, ab_re, ab_im, cre_bd, cim_bd, state_in)


def _merge_mlp_kernel(final, g_ref, ya_ref, ys_ref, u_ref, h_ref, sd_ref, wglu_ref, wout_ref,
                      nw_ref, w1_ref, w2_ref, fw_ref, o_ref, xn_ref):
    f = pl.program_id(1)

    @pl.when(f == 0)
    def _():
        ys = ys_ref[...].astype(F32) + sd_ref[...] * u_ref[...].astype(F32)
        cdf = 0.5 * (1.0 + jnp.tanh(math.sqrt(2.0 / math.pi) * (ys + 0.044715 * (ys * ys * ys))))
        vg = _bdot((ys * cdf).astype(BF16), wglu_ref[...])
        yb = vg[:, :D_MODEL] * _sigmoid(vg[:, D_MODEL:])
        g = g_ref[...].astype(F32)
        m = _sigmoid(g[:, :D_MODEL]) * ya_ref[...].astype(F32) + _sigmoid(g[:, D_MODEL:]) * yb
        x = h_ref[...] + _bdot(m.astype(BF16), wout_ref[...])
        xn_ref[...] = (_rms_scale(x) * nw_ref[...]).astype(BF16)
        o_ref[...] = x

    hid = jnp.square(jnp.maximum(_bdot(xn_ref[...], w1_ref[...]), 0.0))
    o_ref[...] += _bdot(hid.astype(BF16), w2_ref[...])

    if final:
        @pl.when(f == pl.num_programs(1) - 1)
        def _():
            o_ref[...] = _rms_scale(o_ref[...]) * fw_ref[...]


def _merge_mlp(oth, y_a, y_s5, h, s5_d, w_glu, w_out, norm_w, w1, w2, final_w, final):
    rows = h.shape[0]
    tm = min(TM_MERGE, rows)
    u_blk = (OTH_W - S5_WIDTH) // S5_WIDTH
    const = lambda i, f: (0, 0)
    row = lambda i, f: (i, 0)
    return pl.pallas_call(
        functools.partial(_merge_mlp_kernel, final),
        grid=(rows // tm, D_FF // TF_MLP),
        in_specs=[
            pl.BlockSpec((tm, 2 * D_MODEL), row),
            pl.BlockSpec((tm, SSD_INNER), row),
            pl.BlockSpec((tm, S5_WIDTH), row),
            pl.BlockSpec((tm, S5_WIDTH), lambda i, f: (i, u_blk)),
            pl.BlockSpec((tm, D_MODEL), row),
            pl.BlockSpec((1, S5_WIDTH), const),
            pl.BlockSpec((S5_WIDTH, 2 * D_MODEL), const),
            pl.BlockSpec((D_MODEL, D_MODEL), const),
            pl.BlockSpec((1, D_MODEL), const),
            pl.BlockSpec((D_MODEL, TF_MLP), lambda i, f: (0, f)),
            pl.BlockSpec((TF_MLP, D_MODEL), lambda i, f: (f, 0)),
            pl.BlockSpec((1, D_MODEL), const),
        ],
        out_specs=pl.BlockSpec((tm, D_MODEL), row),
        out_shape=jax.ShapeDtypeStruct((rows, D_MODEL), F32),
        scratch_shapes=[pltpu.VMEM((tm, D_MODEL), BF16)],
        compiler_params=_params(2),
        name="merge_mlp",
    )(oth, y_a, y_s5, oth, h, s5_d, w_glu, w_out, norm_w, w1, w2, final_w)


def _pad_lanes(v):
    return jnp.pad(v, (0, LANE - v.shape[0])).reshape(1, LANE)


def _block_diag(blocks):
    s, n, r, c = blocks.shape
    eye = jnp.eye(n, dtype=blocks.dtype)
    return (blocks[:, :, :, None, :] * eye[None, :, None, :, None]).reshape(s, n * r, n * c)


def _mixers(dims, h, p, conv_tail, ssd_state, s5_state):
    act3, oth3, dt3, tail = _inproj(
        dims, h.reshape(dims.nb, dims.lp, D_MODEL), p["norm_mix_w"], p["w_x"], p["w_o"], p["w_dt"],
        p["conv_w"], p["conv_b"], conv_tail)
    y_a, st = _ssd(dims, act3, oth3, dt3, p["dt_bias"], p["a_log"], p["d_full"], p["ssd_norm_w"],
                   ssd_state)
    s5_dims = dims._replace(nb=max(dims.nb, SUBLANE))
    y_s5, hs = _s5(s5_dims, jnp.broadcast_to(oth3, (s5_dims.nb,) + oth3.shape[1:]),
                   p["bre_bd"], p["bim_bd"], p["ab_re"], p["ab_im"], p["cre_bd"], p["cim_bd"], s5_state)
    y_s5 = y_s5[:dims.nb]
    rows = dims.nb * dims.lp
    mixed = (oth3.reshape(rows, OTH_W), y_a.reshape(rows, SSD_INNER), y_s5.reshape(rows, S5_WIDTH))
    return mixed, tail, st, hs


def kernel(x, meta_tokens, w_in, conv_w, conv_b, dt_bias, ssd_a_log, ssd_d, ssd_norm_w, s5_a_re, s5_a_im, s5_log_step, s5_b_re, s5_b_im, s5_c_re, s5_c_im, s5_d, w_glu, w_out, norm_mix_w, norm_mlp_w, w_ff_in, w_ff_out, final_norm_w):
    o_xbc = SSD_INNER
    o_dt = o_xbc + XBC_WIDTH
    o_u = o_dt + SSD_HEADS
    o_g = o_u + S5_WIDTH
    n_half = 2
    gph = S5_GROUPS // n_half
    gpc = S5_CTILE // S5_STATE
    final_w = final_norm_w.reshape(1, D_MODEL)

    h = x.reshape(BATCH * SEQ, D_MODEL)
    meta_seq = jnp.concatenate(
        [jnp.zeros((PREFIX.pad, D_MODEL), x.dtype), meta_tokens.astype(x.dtype)], axis=0)
    h_pre = jnp.broadcast_to(meta_seq[None], (PREFIX.nb, PREFIX.lp, D_MODEL)).reshape(-1, D_MODEL)
    zero_tail = jnp.zeros((SUBLANE, XBC_WIDTH), F32)
    zero_ssd = jnp.zeros((SSD_GROUPS, SSD_STATE, SSD_GROUP_W), F32)
    zero_s5 = jnp.zeros((1, 2 * S5_LANES), F32)

    for i in range(DEPTH):
        wi = w_in[i]
        ab_re, ab_im, bb_re_t, bb_im_t = _s5_disc(
            s5_a_re[i].reshape(S5_GROUPS, 1, S5_STATE), s5_a_im[i].reshape(S5_GROUPS, 1, S5_STATE),
            s5_log_step[i].reshape(S5_GROUPS, 1, 1),
            jnp.swapaxes(s5_b_re[i], 1, 2), jnp.swapaxes(s5_b_im[i], 1, 2))
        bd_b = lambda bt: _block_diag(
            bt.reshape(n_half, gph, S5_GROUP, S5_STATE)).astype(BF16)
        bd_c = lambda cm: _block_diag(
            jnp.swapaxes(cm, 1, 2).reshape(S5_GROUPS // gpc, gpc, S5_STATE, S5_GROUP)).astype(BF16)
        p = {
            "norm_mix_w": norm_mix_w[i].reshape(1, D_MODEL),
            "w_x": wi[:, o_xbc:o_dt].astype(BF16),
            "w_o": jnp.concatenate([wi[:, o_g:], wi[:, :o_xbc], wi[:, o_u:o_g]], axis=1).astype(BF16),
            "w_dt": jnp.pad(wi[:, o_dt:o_u], ((0, 0), (0, LANE - SSD_HEADS))).astype(BF16),
            "conv_w": conv_w[i],
            "conv_b": conv_b[i].reshape(1, XBC_WIDTH),
            "dt_bias": _pad_lanes(dt_bias[i]),
            "a_log": _pad_lanes(ssd_a_log[i]),
            "d_full": jnp.repeat(ssd_d[i], SSD_HEAD_DIM).reshape(1, SSD_INNER),
            "ssd_norm_w": ssd_norm_w[i].reshape(1, SSD_INNER),
            "bre_bd": bd_b(bb_re_t),
            "bim_bd": bd_b(bb_im_t),
            "ab_re": ab_re.reshape(1, S5_LANES),
            "ab_im": ab_im.reshape(1, S5_LANES),
            "cre_bd": bd_c(s5_c_re[i]),
            "cim_bd": bd_c(s5_c_im[i]),
            "s5_d": s5_d[i].reshape(1, S5_WIDTH),
            "w_glu": w_glu[i].astype(BF16),
            "w_out": w_out[i].astype(BF16),
        }
        chan_w = (p["s5_d"], p["w_glu"], p["w_out"], norm_mlp_w[i].reshape(1, D_MODEL),
                  w_ff_in[i].astype(BF16), w_ff_out[i].astype(BF16))
        last = i == DEPTH - 1

        mixed, tail, st, hs = _mixers(PREFIX, h_pre, p, zero_tail, zero_ssd, zero_s5)
        if not last:
            h_pre = _merge_mlp(*mixed, h_pre, *chan_w, final_w, False)
        mixed, _, _, _ = _mixers(MAIN, h, p, tail[0, -1], st[0], hs[0:1])
        h = _merge_mlp(*mixed, h, *chan_w, final_w, last)

    return h.reshape(BATCH, SEQ, D_MODEL)
```
